```python
import jax, jax.numpy as jnp
from jax import lax
import numpy as np

D_MODEL = 1024
BATCH = 16
SEQ = 2048
DEPTH = 4

N_MIXERS = 2
EPS = 1e-6
ROPE_BASE = 10000.0

RET_HEADS = 4
RET_DK = 256
RET_DV = 512
RET_CHUNK = 128
RET_QK_W = RET_HEADS * RET_DK
RET_V_W = RET_HEADS * RET_DV
RET_IN_W = 2 * RET_QK_W + 2 * RET_V_W

MLA_HEADS = 16
MLA_Q_RANK = 256
MLA_KV_RANK = 128
MLA_NOPE = 128
MLA_ROPE = 64
MLA_V = 128
MLA_QK = MLA_NOPE + MLA_ROPE
MLA_V_W = MLA_HEADS * MLA_V
MLA_IN_W = MLA_Q_RANK + MLA_KV_RANK + MLA_ROPE + MLA_V_W
Q_BLOCK = 128

kernel_name = "hybrid_retention_mla_gated_trunk"


def _rms(x):
    xf = x.astype(jnp.float32)
    return xf * lax.rsqrt(jnp.mean(xf * xf, axis=-1, keepdims=True) + EPS)


def rmsnorm(x, g):
    return (_rms(x) * g.astype(jnp.float32)).astype(x.dtype)


def rope(x, positions):
    d = x.shape[-1]
    half = d // 2
    inv_freq = ROPE_BASE ** (-jnp.arange(half, dtype=jnp.float32) / half)
    ang = positions.astype(jnp.float32)[..., None] * inv_freq
    cos = jnp.cos(ang)[:, :, None, :]
    sin = jnp.sin(ang)[:, :, None, :]
    xf = x.astype(jnp.float32)
    x1, x2 = xf[..., :half], xf[..., half:]
    out = jnp.concatenate([x1 * cos - x2 * sin, x2 * cos + x1 * sin], axis=-1)
    return out.astype(x.dtype)


def retention_mixer(h, positions, w_in, gn, w_out):
    B, S, _ = h.shape
    H, C = RET_HEADS, RET_CHUNK
    nc = S // C
    proj = h @ w_in
    q, k, v, gate = jnp.split(proj, [RET_QK_W, 2 * RET_QK_W, 2 * RET_QK_W + RET_V_W], axis=-1)
    q = rope(q.reshape(B, S, H, RET_DK), positions)
    k = rope(k.reshape(B, S, H, RET_DK), positions) * (RET_DK ** -0.5)
    v = v.reshape(B, S, H, RET_DV)

    log_g = jnp.log(1.0 - 2.0 ** (-5.0 - jnp.arange(H, dtype=jnp.float32)))
    idx = jnp.arange(C, dtype=jnp.float32)
    diff = idx[:, None] - idx[None, :]
    dmask = jnp.where(diff[None] >= 0, jnp.exp(jnp.maximum(diff, 0.0)[None] * log_g[:, None, None]), 0.0)
    q_decay = jnp.exp((idx + 1.0)[:, None] * log_g[None, :])
    k_decay = jnp.exp((C - 1.0 - idx)[:, None] * log_g[None, :])
    chunk_decay = jnp.exp(C * log_g)

    qc = q.reshape(B, nc, C, H, RET_DK)
    kc = k.reshape(B, nc, C, H, RET_DK)
    vc = v.reshape(B, nc, C, H, RET_DV)

    s = jnp.einsum('bnihd,bnjhd->bnhij', qc, kc).astype(jnp.float32) * dmask[None, None]
    intra = jnp.einsum('bnhij,bnjhv->bnihv', s.astype(v.dtype), vc)

    def body(R, inp):
        q_n, k_n, v_n = inp
        cross = jnp.einsum('bihd,bhdv->bihv', q_n.astype(jnp.float32), R) * q_decay[None, :, :, None]
        kv = jnp.einsum('bjhd,bjhv->bhdv',
                        k_n.astype(jnp.float32) * k_decay[None, :, :, None],
                        v_n.astype(jnp.float32))
        R = R * chunk_decay[None, :, None, None] + kv
        return R, cross.astype(v_n.dtype)

    R0 = jnp.zeros((B, H, RET_DK, RET_DV), jnp.float32)
    xs = (qc.transpose(1, 0, 2, 3, 4), kc.transpose(1, 0, 2, 3, 4), vc.transpose(1, 0, 2, 3, 4))
    _, cross = lax.scan(body, R0, xs)
    o = (intra + cross.transpose(1, 0, 2, 3, 4)).reshape(B, S, H, RET_DV)

    o = (_rms(o).reshape(B, S, RET_V_W) * gn.astype(jnp.float32)).astype(h.dtype)
    return (o * jax.nn.silu(gate)) @ w_out


def causal_block_attention(q, k, v):
    B, S, H, dq = q.shape
    nb = S // Q_BLOCK
    scale = dq ** -0.5
    qb = q.reshape(B, nb, Q_BLOCK, H, dq).transpose(1, 0, 2, 3, 4)
    k_pos = jnp.arange(S)

    def one_block(args):
        q_blk, blk = args
        s = jnp.einsum('bqhd,bkhd->bhqk', q_blk, k).astype(jnp.float32) * scale
        q_pos = blk * Q_BLOCK + jnp.arange(Q_BLOCK)
        s = jnp.where(k_pos[None, :] <= q_pos[:, None], s, -1e30)
        p = jax.nn.softmax(s, axis=-1).astype(v.dtype)
        return jnp.einsum('bhqk,bkhv->bqhv', p, v)

    out = lax.map(one_block, (qb, jnp.arange(nb)))
    return out.transpose(1, 0, 2, 3, 4).reshape(B, S, H, v.shape[-1])


def mla_mixer(h, positions, w_in, q_norm, w_q_b, kv_norm, w_kv_b, w_out):
    B, S, _ = h.shape
    H = MLA_HEADS
    proj = h @ w_in
    q_lat, kv_lat, k_rope, gate = jnp.split(
        proj, [MLA_Q_RANK, MLA_Q_RANK + MLA_KV_RANK, MLA_Q_RANK + MLA_KV_RANK + MLA_ROPE], axis=-1)
    q = (rmsnorm(q_lat, q_norm) @ w_q_b).reshape(B, S, H, MLA_QK)
    q = jnp.concatenate([q[..., :MLA_NOPE], rope(q[..., MLA_NOPE:], positions)], axis=-1)
    kv = (rmsnorm(kv_lat, kv_norm) @ w_kv_b).reshape(B, S, H, MLA_NOPE + MLA_V)
    k_nope, v = kv[..., :MLA_NOPE], kv[..., MLA_NOPE:]
    k_rope = rope(k_rope[:, :, None, :], positions)
    k = jnp.concatenate([k_nope, jnp.broadcast_to(k_rope, (B, S, H, MLA_ROPE))], axis=-1)
    o = causal_block_attention(q, k, v).reshape(B, S, MLA_V_W)
    return (o * jax.nn.silu(gate)) @ w_out


def setup_inputs(seed: int = 0) -> dict:
    key = jax.random.key(seed)
    keys = iter(jax.random.split(key, 64))

    def w(shape, fan_in):
        return jax.random.normal(next(keys), shape, jnp.float32) * (fan_in ** -0.5)

    def gain(n):
        return 1.0 + 0.02 * jax.random.normal(next(keys), (n,), jnp.float32)

    x = jax.random.normal(next(keys), (BATCH, SEQ, D_MODEL), jnp.float32)
    offset = jax.random.randint(next(keys), (BATCH, 1), 0, 1024, dtype=jnp.int32)
    positions = (offset + jnp.arange(SEQ, dtype=jnp.int32)[None, :]).astype(jnp.int32)

    d = {"x": x, "positions": positions}
    for i in range(DEPTH):
        d[f"l{i}_norm"] = gain(D_MODEL)
        if i % N_MIXERS == 0:
            d[f"l{i}_ret_w_in"] = w((D_MODEL, RET_IN_W), D_MODEL)
            d[f"l{i}_ret_gn"] = gain(RET_V_W)
            d[f"l{i}_ret_w_out"] = w((RET_V_W, D_MODEL), RET_V_W)
        else:
            d[f"l{i}_mla_w_in"] = w((D_MODEL, MLA_IN_W), D_MODEL)
            d[f"l{i}_mla_q_norm"] = gain(MLA_Q_RANK)
            d[f"l{i}_mla_w_q_b"] = w((MLA_Q_RANK, MLA_HEADS * MLA_QK), MLA_Q_RANK)
            d[f"l{i}_mla_kv_norm"] = gain(MLA_KV_RANK)
            d[f"l{i}_mla_w_kv_b"] = w((MLA_KV_RANK, MLA_HEADS * (MLA_NOPE + MLA_V)), MLA_KV_RANK)
            d[f"l{i}_mla_w_out"] = w((MLA_V_W, D_MODEL), MLA_V_W)
    d["final_norm"] = gain(D_MODEL)
    return d


def reference(x, positions,
              l0_norm, l0_ret_w_in, l0_ret_gn, l0_ret_w_out,
              l1_norm, l1_mla_w_in, l1_mla_q_norm, l1_mla_w_q_b, l1_mla_kv_norm, l1_mla_w_kv_b, l1_mla_w_out,
              l2_norm, l2_ret_w_in, l2_ret_gn, l2_ret_w_out,
              l3_norm, l3_mla_w_in, l3_mla_q_norm, l3_mla_w_q_b, l3_mla_kv_norm, l3_mla_w_kv_b, l3_mla_w_out,
              final_norm):
    norms = [l0_norm, l1_norm, l2_norm, l3_norm]
    ret_params = [(l0_ret_w_in, l0_ret_gn, l0_ret_w_out),
                  (l2_ret_w_in, l2_ret_gn, l2_ret_w_out)]
    mla_params = [(l1_mla_w_in, l1_mla_q_norm, l1_mla_w_q_b, l1_mla_kv_norm, l1_mla_w_kv_b, l1_mla_w_out),
                  (l3_mla_w_in, l3_mla_q_norm, l3_mla_w_q_b, l3_mla_kv_norm, l3_mla_w_kv_b, l3_mla_w_out)]
    h = x
    for i in range(DEPTH):
        u = rmsnorm(h, norms[i])
        if i % N_MIXERS == 0:
            h = h + retention_mixer(u, positions, *ret_params[i // N_MIXERS])
        else:
            h = h + mla_mixer(u, positions, *mla_params[i // N_MIXERS])
    return rmsnorm(h, final_norm)
```

```python
import functools
import math

import jax
import jax.numpy as jnp
from jax import lax
from jax.experimental import pallas as pl
from jax.experimental.pallas import tpu as pltpu

D_MODEL = 1024
EPS = 1e-6
ROPE_BASE = 10000.0

RET_HEADS = 4
RET_DK = 256
RET_DV = 512
RET_CHUNK = 128
RET_QK_W = RET_HEADS * RET_DK
RET_V_W = RET_HEADS * RET_DV

MLA_HEADS = 16
MLA_Q_RANK = 256
MLA_KV_RANK = 128
MLA_NOPE = 128
MLA_ROPE = 64
MLA_V = 128
MLA_QK = MLA_NOPE + MLA_ROPE
MLA_V_W = MLA_HEADS * MLA_V
MLA_LAT_W = MLA_Q_RANK + MLA_KV_RANK + MLA_ROPE
MLA_LAT_PAD = 512
MLA_HEAD_PAD = 256

LANES = 128
VMEM_LIMIT = 56 * 1024 * 1024

ROW_TILE = 512
RET_ROWS = 256
ATT_TILE = 512

BF16 = jnp.bfloat16
F32 = jnp.float32

ATT_Q_SCALE = (MLA_QK ** -0.5) * math.log2(math.e)
MASK_VALUE = -1e30


def _dot(a, b):
    return jnp.dot(a, b, preferred_element_type=F32)


def _dot_nt(a, b):
    return lax.dot_general(a, b, (((1,), (1,)), ((), ())), preferred_element_type=F32)


def _dot_tn(a, b):
    return lax.dot_general(a, b, (((0,), (0,)), ((), ())), preferred_element_type=F32)


def _rms(x):
    return x * lax.rsqrt(jnp.mean(x * x, axis=-1, keepdims=True) + EPS)


def _silu(g):
    return g * (1.0 / (1.0 + jnp.exp(-g)))


def _resident(shape):
    return pl.BlockSpec(shape, lambda *_: (0,) * len(shape), pipeline_mode=pl.Buffered(1))


def _params(*semantics):
    return pltpu.CompilerParams(dimension_semantics=semantics, vmem_limit_bytes=VMEM_LIMIT)


def _rope_table_kernel(pos_ref, inv_r_ref, inv_m_ref, sign_m_ref,
                       cos_r_ref, sin_r_ref, cos_m_ref, sin_m_ref):
    pos = pos_ref[...].astype(F32)
    ang_r = pos * inv_r_ref[...]
    cos_r_ref[...] = jnp.cos(ang_r)
    sin_r_ref[...] = jnp.sin(ang_r)
    ang_m = pos * inv_m_ref[...]
    cos_m_ref[...] = jnp.cos(ang_m)
    sin_m_ref[...] = jnp.sin(ang_m) * sign_m_ref[...]


def _rope_tables(positions):
    n = positions.size
    rows = 2048
    pos = positions.reshape(n, 1)
    half_r = RET_DK // 2
    half_m = MLA_ROPE // 2
    inv_r = (ROPE_BASE ** (-jnp.arange(half_r, dtype=F32) / half_r)).reshape(1, LANES)
    inv_m_half = ROPE_BASE ** (-jnp.arange(half_m, dtype=F32) / half_m)
    zeros = jnp.zeros((LANES - 2 * half_m,), F32)
    inv_m = jnp.concatenate([inv_m_half, inv_m_half, zeros]).reshape(1, LANES)
    sign_m = jnp.concatenate([-jnp.ones((half_m,), F32), jnp.ones((half_m,), F32), zeros]).reshape(1, LANES)
    const = pl.BlockSpec((1, LANES), lambda i: (0, 0))
    tab = pl.BlockSpec((rows, LANES), lambda i: (i, 0))
    out = jax.ShapeDtypeStruct((n, LANES), F32)
    return pl.pallas_call(
        _rope_table_kernel,
        grid=(n // rows,),
        in_specs=[pl.BlockSpec((rows, 1), lambda i: (i, 0)), const, const, const],
        out_specs=[tab, tab, tab, tab],
        out_shape=[out, out, out, out],
        compiler_params=_params("parallel"),
        name="rope_tables",
    )(pos, inv_r, inv_m, sign_m)


def _ret_inproj_kernel(h_ref, g_ref, w_ref, cos_ref, sin_ref, q_ref, k_ref, v_ref, gate_ref):
    xn = (_rms(h_ref[...]) * g_ref[...]).astype(BF16)
    cos = cos_ref[...]
    sin = sin_ref[...]
    half = RET_DK // 2

    def roped(col0, scale, out_ref):
        for hd in range(RET_HEADS):
            c = hd * RET_DK
            r = _dot(xn, w_ref[:, col0 + c:col0 + c + RET_DK])
            x1 = r[:, :half]
            x2 = r[:, half:]
            out_ref[:, c:c + half] = ((x1 * cos - x2 * sin) * scale).astype(BF16)
            out_ref[:, c + half:c + RET_DK] = ((x2 * cos + x1 * sin) * scale).astype(BF16)

    roped(0, 1.0, q_ref)
    roped(RET_QK_W, RET_DK ** -0.5, k_ref)
    for hd in range(RET_HEADS):
        c = hd * RET_DV
        v_ref[:, c:c + RET_DV] = _dot(xn, w_ref[:, 2 * RET_QK_W + c:2 * RET_QK_W + c + RET_DV]).astype(BF16)
        g0 = 2 * RET_QK_W + RET_V_W + c
        gate_ref[:, c:c + RET_DV] = _dot(xn, w_ref[:, g0:g0 + RET_DV]).astype(BF16)


def _ret_inproj(h, gain, w_in, cos_r, sin_r):
    n = h.shape[0]
    rows = ROW_TILE
    in_w = w_in.shape[1]
    row_block = lambda w: pl.BlockSpec((rows, w), lambda i: (i, 0))
    return pl.pallas_call(
        _ret_inproj_kernel,
        grid=(n // rows,),
        in_specs=[row_block(D_MODEL), _resident((1, D_MODEL)), _resident((D_MODEL, in_w)),
                  row_block(LANES), row_block(LANES)],
        out_specs=[row_block(RET_QK_W), row_block(RET_QK_W), row_block(RET_V_W), row_block(RET_V_W)],
        out_shape=[jax.ShapeDtypeStruct((n, RET_QK_W), BF16), jax.ShapeDtypeStruct((n, RET_QK_W), BF16),
                   jax.ShapeDtypeStruct((n, RET_V_W), BF16), jax.ShapeDtypeStruct((n, RET_V_W), BF16)],
        compiler_params=_params("parallel"),
        name="ret_inproj",
    )(h, gain.reshape(1, D_MODEL), w_in.astype(BF16), cos_r, sin_r)


def _ret_core_kernel(cd_ref, q_ref, k_ref, v_ref, gate_ref, gn_ref, dmask_ref, qdec_ref, kdec_ref,
                     y_ref, state_ref):
    @pl.when(pl.program_id(1) == 0)
    def _():
        state_ref[...] = jnp.zeros_like(state_ref)

    lane_tiles_v = RET_DV // LANES
    lane_tiles_k = RET_DK // LANES
    for c in range(RET_ROWS // RET_CHUNK):
        r0 = c * RET_CHUNK
        for hd in range(RET_HEADS):
            qc = q_ref[r0:r0 + RET_CHUNK, hd * RET_DK:(hd + 1) * RET_DK]
            kc = k_ref[r0:r0 + RET_CHUNK, hd * RET_DK:(hd + 1) * RET_DK]
            vc = v_ref[r0:r0 + RET_CHUNK, hd * RET_DV:(hd + 1) * RET_DV]
            qdec = qdec_ref[hd]
            kdec = kdec_ref[hd]
            s = (_dot_nt(qc, kc) * dmask_ref[hd]).astype(BF16)
            intra = _dot(s, vc)
            state = state_ref[hd]
            cross = _dot(qc, state.astype(BF16))
            cross = jnp.concatenate(
                [cross[:, t * LANES:(t + 1) * LANES] * qdec for t in range(lane_tiles_v)], axis=1)
            kf = kc.astype(F32)
            kd = jnp.concatenate(
                [kf[:, t * LANES:(t + 1) * LANES] * kdec for t in range(lane_tiles_k)], axis=1).astype(BF16)
            state_ref[hd] = state * cd_ref[hd] + _dot_tn(kd, vc)
            o = _rms(intra + cross) * gn_ref[:, hd * RET_DV:(hd + 1) * RET_DV]
            gate = gate_ref[r0:r0 + RET_CHUNK, hd * RET_DV:(hd + 1) * RET_DV].astype(F32)
            y_ref[r0:r0 + RET_CHUNK, hd * RET_DV:(hd + 1) * RET_DV] = (o * _silu(gate)).astype(BF16)


def _ret_core(q, k, v, gate, gn, batch, seq):
    n = q.shape[0]
    steps = seq // RET_ROWS
    heads = jnp.arange(RET_HEADS, dtype=F32)
    log_g = jnp.log(1.0 - 2.0 ** (-5.0 - heads))
    idx = jnp.arange(RET_CHUNK, dtype=F32)
    diff = idx[:, None] - idx[None, :]
    dmask = jnp.where(diff[None] >= 0, jnp.exp(jnp.maximum(diff, 0.0)[None] * log_g[:, None, None]), 0.0)
    q_decay = jnp.exp((idx + 1.0)[None, :] * log_g[:, None])
    k_decay = jnp.exp((RET_CHUNK - 1.0 - idx)[None, :] * log_g[:, None])
    chunk_decay = jnp.exp(RET_CHUNK * log_g)
    qdec = jnp.broadcast_to(q_decay[:, :, None], (RET_HEADS, RET_CHUNK, LANES))
    kdec = jnp.broadcast_to(k_decay[:, :, None], (RET_HEADS, RET_CHUNK, LANES))
    row_block = lambda w: pl.BlockSpec((RET_ROWS, w), lambda b, i: (b * steps + i, 0))
    table = _resident((RET_HEADS, RET_CHUNK, LANES))
    return pl.pallas_call(
        _ret_core_kernel,
        grid=(batch, steps),
        in_specs=[pl.BlockSpec(memory_space=pltpu.SMEM),
                  row_block(RET_QK_W), row_block(RET_QK_W), row_block(RET_V_W), row_block(RET_V_W),
                  _resident((1, RET_V_W)), _resident((RET_HEADS, RET_CHUNK, RET_CHUNK)), table, table],
        out_specs=row_block(RET_V_W),
        out_shape=jax.ShapeDtypeStruct((n, RET_V_W), BF16),
        scratch_shapes=[pltpu.VMEM((RET_HEADS, RET_DK, RET_DV), F32)],
        compiler_params=_params("parallel", "arbitrary"),
        name="ret_core",
    )(chunk_decay, q, k, v, gate, gn.reshape(1, RET_V_W), dmask, qdec, kdec)


def _mla_inproj_kernel(h_ref, g_ref, wlat_ref, wgate_ref, qn_ref, kvn_ref, wq_ref, wk_ref, wv_ref,
                       cos_ref, sin_ref, q_ref, k_ref, v_ref, gate_ref):
    xn = (_rms(h_ref[...]) * g_ref[...]).astype(BF16)
    lat = _dot(xn, wlat_ref[...])
    q_lat = lat[:, :MLA_Q_RANK]
    kv_lat = lat[:, MLA_Q_RANK:MLA_Q_RANK + MLA_KV_RANK]
    k_rope = lat[:, MLA_Q_RANK + MLA_KV_RANK:]
    qn = (_rms(q_lat) * qn_ref[...]).astype(BF16)
    kvn = (_rms(kv_lat) * kvn_ref[...]).astype(BF16)
    cos = cos_ref[...]
    sin = sin_ref[...]

    def rope(x):
        swapped = pltpu.roll(x, MLA_ROPE // 2, 1) + pltpu.roll(x, LANES - MLA_ROPE // 2, 1)
        return x * cos + swapped * sin

    k_rope = rope(k_rope).astype(BF16)
    group = 4
    for g in range(MLA_HEADS // group):
        qg = _dot(qn, wq_ref[:, g * group * MLA_HEAD_PAD:(g + 1) * group * MLA_HEAD_PAD])
        kg = _dot(kvn, wk_ref[:, g * group * MLA_NOPE:(g + 1) * group * MLA_NOPE])
        for j in range(group):
            c = (g * group + j) * MLA_HEAD_PAD
            q_nope = qg[:, j * MLA_HEAD_PAD:j * MLA_HEAD_PAD + MLA_NOPE]
            q_rope = qg[:, j * MLA_HEAD_PAD + MLA_NOPE:(j + 1) * MLA_HEAD_PAD]
            q_ref[:, c:c + MLA_NOPE] = (q_nope * ATT_Q_SCALE).astype(BF16)
            q_ref[:, c + MLA_NOPE:c + MLA_HEAD_PAD] = (rope(q_rope) * ATT_Q_SCALE).astype(BF16)
            k_ref[:, c:c + MLA_NOPE] = kg[:, j * MLA_NOPE:(j + 1) * MLA_NOPE].astype(BF16)
            k_ref[:, c + MLA_NOPE:c + MLA_HEAD_PAD] = k_rope
        w = group * MLA_V
        v_ref[:, g * w:(g + 1) * w] = _dot(kvn, wv_ref[:, g * w:(g + 1) * w]).astype(BF16)
        gate_ref[:, g * w:(g + 1) * w] = _dot(xn, wgate_ref[:, g * w:(g + 1) * w]).astype(BF16)


def _mla_inproj(h, gain, w_in, q_norm, w_q_b, kv_norm, w_kv_b, cos_m, sin_m):
    n = h.shape[0]
    rows = ROW_TILE
    w_lat = jnp.pad(w_in[:, :MLA_LAT_W], ((0, 0), (0, MLA_LAT_PAD - MLA_LAT_W))).astype(BF16)
    w_gate = w_in[:, MLA_LAT_W:].astype(BF16)
    w_q = jnp.pad(w_q_b.reshape(MLA_Q_RANK, MLA_HEADS, MLA_QK),
                  ((0, 0), (0, 0), (0, MLA_HEAD_PAD - MLA_QK))).reshape(MLA_Q_RANK, MLA_HEADS * MLA_HEAD_PAD)
    w_kv = w_kv_b.reshape(MLA_KV_RANK, MLA_HEADS, MLA_NOPE + MLA_V)
    w_k = w_kv[:, :, :MLA_NOPE].reshape(MLA_KV_RANK, MLA_HEADS * MLA_NOPE)
    w_v = w_kv[:, :, MLA_NOPE:].reshape(MLA_KV_RANK, MLA_V_W)
    qk_w = MLA_HEADS * MLA_HEAD_PAD
    row_block = lambda w: pl.BlockSpec((rows, w), lambda i: (i, 0))
    return pl.pallas_call(
        _mla_inproj_kernel,
        grid=(n // rows,),
        in_specs=[row_block(D_MODEL), _resident((1, D_MODEL)),
                  _resident((D_MODEL, MLA_LAT_PAD)), _resident((D_MODEL, MLA_V_W)),
                  _resident((1, MLA_Q_RANK)), _resident((1, MLA_KV_RANK)),
                  _resident((MLA_Q_RANK, qk_w)), _resident((MLA_KV_RANK, MLA_HEADS * MLA_NOPE)),
                  _resident((MLA_KV_RANK, MLA_V_W)),
                  row_block(LANES), row_block(LANES)],
        out_specs=[row_block(qk_w), row_block(qk_w), row_block(MLA_V_W), row_block(MLA_V_W)],
        out_shape=[jax.ShapeDtypeStruct((n, qk_w), BF16), jax.ShapeDtypeStruct((n, qk_w), BF16),
                   jax.ShapeDtypeStruct((n, MLA_V_W), BF16), jax.ShapeDtypeStruct((n, MLA_V_W), BF16)],
        compiler_params=_params("parallel"),
        name="mla_inproj",
    )(h, gain.reshape(1, D_MODEL), w_lat, w_gate, q_norm.reshape(1, MLA_Q_RANK),
      kv_norm.reshape(1, MLA_KV_RANK), w_q.astype(BF16), w_k.astype(BF16), w_v.astype(BF16), cos_m, sin_m)


def _mla_attn_kernel(q_ref, k_ref, v_ref, gate_ref, y_ref):
    qi = pl.program_id(2)
    q = q_ref[...]

    def block(j, carry, masked):
        m, l, acc = carry
        kb = k_ref[pl.ds(pl.multiple_of(j * ATT_TILE, ATT_TILE), ATT_TILE), :]
        vb = v_ref[pl.ds(pl.multiple_of(j * ATT_TILE, ATT_TILE), ATT_TILE), :]
        s = _dot_nt(q, kb)
        if masked:
            row = lax.broadcasted_iota(jnp.int32, s.shape, 0)
            col = lax.broadcasted_iota(jnp.int32, s.shape, 1)
            s = jnp.where(col <= row, s, MASK_VALUE)
        m_new = jnp.maximum(m, jnp.max(s, axis=-1, keepdims=True))
        p = jnp.exp2(s - m_new)
        alpha = jnp.exp2(m - m_new)
        l = alpha * l + jnp.sum(p, axis=-1, keepdims=True)
        acc = alpha * acc + _dot(p.astype(BF16), vb)
        return m_new, l, acc

    init = (jnp.full((ATT_TILE, 1), MASK_VALUE, F32), jnp.zeros((ATT_TILE, 1), F32),
            jnp.zeros((ATT_TILE, MLA_V), F32))
    carry = lax.fori_loop(0, qi, lambda j, c: block(j, c, False), init)
    _, l, acc = block(qi, carry, True)
    y_ref[...] = ((acc / l) * _silu(gate_ref[...].astype(F32))).astype(BF16)


def _mla_attn(q, k, v, gate, batch, seq):
    n = q.shape[0]
    nq = seq // ATT_TILE
    return pl.pallas_call(
        _mla_attn_kernel,
        grid=(batch, MLA_HEADS, nq),
        in_specs=[pl.BlockSpec((ATT_TILE, MLA_HEAD_PAD), lambda b, h, i: (b * nq + i, h)),
                  pl.BlockSpec((seq, MLA_HEAD_PAD), lambda b, h, i: (b, h)),
                  pl.BlockSpec((seq, MLA_V), lambda b, h, i: (b, h)),
                  pl.BlockSpec((ATT_TILE, MLA_V), lambda b, h, i: (b * nq + i, h))],
        out_specs=pl.BlockSpec((ATT_TILE, MLA_V), lambda b, h, i: (b * nq + i, h)),
        out_shape=jax.ShapeDtypeStruct((n, MLA_V_W), BF16),
        compiler_params=_params("parallel", "parallel", "arbitrary"),
        name="mla_attn",
    )(q, k, v, gate)


def _out_proj_kernel(y_ref, w_ref, h_ref, o_ref):
    o_ref[...] = h_ref[...] + _dot(y_ref[...], w_ref[...])


def _out_proj_norm_kernel(y_ref, w_ref, h_ref, g_ref, o_ref):
    o_ref[...] = _rms(h_ref[...] + _dot(y_ref[...], w_ref[...])) * g_ref[...]


def _out_proj(y, w_out, h, final_gain=None):
    n, width = y.shape
    rows = ROW_TILE
    row_block = lambda w: pl.BlockSpec((rows, w), lambda i: (i, 0))
    in_specs = [row_block(width), _resident((width, D_MODEL)), row_block(D_MODEL)]
    args = [y, w_out.astype(BF16), h]
    body = _out_proj_kernel
    if final_gain is not None:
        in_specs.append(_resident((1, D_MODEL)))
        args.append(final_gain.reshape(1, D_MODEL))
        body = _out_proj_norm_kernel
    return pl.pallas_call(
        body,
        grid=(n // rows,),
        in_specs=in_specs,
        out_specs=row_block(D_MODEL),
        out_shape=jax.ShapeDtypeStruct((n, D_MODEL), F32),
        compiler_params=_params("parallel"),
        name="out_proj",
    )(*args)


def kernel(x, positions, l0_norm, l0_ret_w_in, l0_ret_gn, l0_ret_w_out, l1_norm, l1_mla_w_in, l1_mla_q_norm, l1_mla_w_q_b, l1_mla_kv_norm, l1_mla_w_kv_b, l1_mla_w_out, l2_norm, l2_ret_w_in, l2_ret_gn, l2_ret_w_out, l3_norm, l3_mla_w_in, l3_mla_q_norm, l3_mla_w_q_b, l3_mla_kv_norm, l3_mla_w_kv_b, l3_mla_w_out, final_norm):
    batch, seq, d = x.shape
    assert d == D_MODEL and seq % ATT_TILE == 0 and seq % RET_ROWS == 0 and (batch * seq) % ROW_TILE == 0
    h = x.reshape(batch * seq, d)
    cos_r, sin_r, cos_m, sin_m = _rope_tables(positions)

    def retention(h, norm, w_in, gn, w_out):
        q, k, v, gate = _ret_inproj(h, norm, w_in, cos_r, sin_r)
        y = _ret_core(q, k, v, gate, gn, batch, seq)
        return _out_proj(y, w_out, h)

    def latent_attention(h, norm, w_in, q_norm, w_q_b, kv_norm, w_kv_b, w_out, final_gain=None):
        q, k, v, gate = _mla_inproj(h, norm, w_in, q_norm, w_q_b, kv_norm, w_kv_b, cos_m, sin_m)
        y = _mla_attn(q, k, v, gate, batch, seq)
        return _out_proj(y, w_out, h, final_gain)

    h = retention(h, l0_norm, l0_ret_w_in, l0_ret_gn, l0_ret_w_out)
    h = latent_attention(h, l1_norm, l1_mla_w_in, l1_mla_q_norm, l1_mla_w_q_b, l1_mla_kv_norm,
                         l1_mla_w_kv_b, l1_mla_w_out)
    h = retention(h, l2_norm, l2_ret_w_in, l2_ret_gn, l2_ret_w_out)
    h = latent_attention(h, l3_norm, l3_mla_w_in, l3_mla_q_norm, l3_mla_w_q_b, l3_mla_kv_norm,
                         l3_mla_w_kv_b, l3_mla_w_out, final_gain=final_norm)
    return h.reshape(batch, seq, d)
```

```python
import functools
import math

import jax
import jax.numpy as jnp
from jax import lax
from jax.experimental import pallas as pl
from jax.experimental.pallas import tpu as pltpu

D_MODEL = 1024
EPS = 1e-6
ROPE_BASE = 10000.0

RET_HEADS = 4
RET_DK = 256
RET_DV = 512
RET_CHUNK = 128
RET_QK_W = RET_HEADS * RET_DK
RET_V_W = RET_HEADS * RET_DV

MLA_HEADS = 16
MLA_Q_RANK = 256
MLA_KV_RANK = 128
MLA_NOPE = 128
MLA_ROPE = 64
MLA_V = 128
MLA_QK = MLA_NOPE + MLA_ROPE
MLA_V_W = MLA_HEADS * MLA_V
MLA_LAT_W = MLA_Q_RANK + MLA_KV_RANK + MLA_ROPE
MLA_LAT_PAD = 512
MLA_HEAD_PAD = 256

LANES = 128
VMEM_LIMIT = 56 * 1024 * 1024

ROW_TILE = 512
RET_ROWS = 256
ATT_Q_TILE = 512
ATT_K_TILE = 512
ATT_PREFETCH = 2

BF16 = jnp.bfloat16
F32 = jnp.float32

ATT_Q_SCALE = (MLA_QK ** -0.5) * math.log2(math.e)
MASK_VALUE = -1e30


def _dot(a, b):
    return jnp.dot(a, b, preferred_element_type=F32)


def _dot_nt(a, b):
    return lax.dot_general(a, b, (((1,), (1,)), ((), ())), preferred_element_type=F32)


def _dot_tn(a, b):
    return lax.dot_general(a, b, (((0,), (0,)), ((), ())), preferred_element_type=F32)


def _rms(x):
    return x * lax.rsqrt(jnp.mean(x * x, axis=-1, keepdims=True) + EPS)


def _silu(g):
    return g * (1.0 / (1.0 + jnp.exp(-g)))


def _resident(shape):
    return pl.BlockSpec(shape, lambda *_: (0,) * len(shape), pipeline_mode=pl.Buffered(1))


def _params(*semantics, flags=None):
    return pltpu.CompilerParams(dimension_semantics=semantics, vmem_limit_bytes=VMEM_LIMIT, flags=flags)


def _rope_table_kernel(pos_ref, inv_r_ref, inv_m_ref, sign_m_ref,
                       cos_r_ref, sin_r_ref, cos_m_ref, sin_m_ref):
    pos = pos_ref[...].astype(F32)
    ang_r = pos * inv_r_ref[...]
    cos_r_ref[...] = jnp.cos(ang_r)
    sin_r_ref[...] = jnp.sin(ang_r)
    ang_m = pos * inv_m_ref[...]
    cos_m_ref[...] = jnp.cos(ang_m)
    sin_m_ref[...] = jnp.sin(ang_m) * sign_m_ref[...]


def _rope_tables(positions):
    n = positions.size
    rows = 2048
    pos = positions.reshape(n, 1)
    half_r = RET_DK // 2
    half_m = MLA_ROPE // 2
    inv_r = (ROPE_BASE ** (-jnp.arange(half_r, dtype=F32) / half_r)).reshape(1, LANES)
    inv_m_half = ROPE_BASE ** (-jnp.arange(half_m, dtype=F32) / half_m)
    zeros = jnp.zeros((LANES - 2 * half_m,), F32)
    inv_m = jnp.concatenate([inv_m_half, inv_m_half, zeros]).reshape(1, LANES)
    sign_m = jnp.concatenate([-jnp.ones((half_m,), F32), jnp.ones((half_m,), F32), zeros]).reshape(1, LANES)
    const = pl.BlockSpec((1, LANES), lambda i: (0, 0))
    tab = pl.BlockSpec((rows, LANES), lambda i: (i, 0))
    out = jax.ShapeDtypeStruct((n, LANES), F32)
    return pl.pallas_call(
        _rope_table_kernel,
        grid=(n // rows,),
        in_specs=[pl.BlockSpec((rows, 1), lambda i: (i, 0)), const, const, const],
        out_specs=[tab, tab, tab, tab],
        out_shape=[out, out, out, out],
        compiler_params=_params("parallel"),
        name="rope_tables",
    )(pos, inv_r, inv_m, sign_m)


def _ret_inproj_kernel(h_ref, g_ref, w_ref, cos_ref, sin_ref, q_ref, k_ref, v_ref, gate_ref):
    xn = (_rms(h_ref[...]) * g_ref[...]).astype(BF16)
    cos = cos_ref[...]
    sin = sin_ref[...]
    half = RET_DK // 2

    def roped(col0, scale, out_ref):
        for hd in range(RET_HEADS):
            c = hd * RET_DK
            r = _dot(xn, w_ref[:, col0 + c:col0 + c + RET_DK])
            x1 = r[:, :half]
            x2 = r[:, half:]
            out_ref[:, c:c + half] = ((x1 * cos - x2 * sin) * scale).astype(BF16)
            out_ref[:, c + half:c + RET_DK] = ((x2 * cos + x1 * sin) * scale).astype(BF16)

    roped(0, 1.0, q_ref)
    roped(RET_QK_W, RET_DK ** -0.5, k_ref)
    for hd in range(RET_HEADS):
        c = hd * RET_DV
        v_ref[:, c:c + RET_DV] = _dot(xn, w_ref[:, 2 * RET_QK_W + c:2 * RET_QK_W + c + RET_DV]).astype(BF16)
        g0 = 2 * RET_QK_W + RET_V_W + c
        gate_ref[:, c:c + RET_DV] = _dot(xn, w_ref[:, g0:g0 + RET_DV]).astype(BF16)


def _ret_inproj(h, gain, w_in, cos_r, sin_r):
    n = h.shape[0]
    rows = ROW_TILE
    in_w = w_in.shape[1]
    row_block = lambda w: pl.BlockSpec((rows, w), lambda i: (i, 0))
    return pl.pallas_call(
        _ret_inproj_kernel,
        grid=(n // rows,),
        in_specs=[row_block(D_MODEL), _resident((1, D_MODEL)), _resident((D_MODEL, in_w)),
                  row_block(LANES), row_block(LANES)],
        out_specs=[row_block(RET_QK_W), row_block(RET_QK_W), row_block(RET_V_W), row_block(RET_V_W)],
        out_shape=[jax.ShapeDtypeStruct((n, RET_QK_W), BF16), jax.ShapeDtypeStruct((n, RET_QK_W), BF16),
                   jax.ShapeDtypeStruct((n, RET_V_W), BF16), jax.ShapeDtypeStruct((n, RET_V_W), BF16)],
        compiler_params=_params("parallel"),
        name="ret_inproj",
    )(h, gain.reshape(1, D_MODEL), w_in.astype(BF16), cos_r, sin_r)


def _ret_core_kernel(cd_ref, q_ref, k_ref, v_ref, gate_ref, gn_ref, dmask_ref, qdec_ref, kdec_ref,
                     y_ref, state_ref):
    @pl.when(pl.program_id(1) == 0)
    def _():
        state_ref[...] = jnp.zeros_like(state_ref)

    lane_tiles_v = RET_DV // LANES
    lane_tiles_k = RET_DK // LANES
    for c in range(RET_ROWS // RET_CHUNK):
        r0 = c * RET_CHUNK
        for hd in range(RET_HEADS):
            qc = q_ref[r0:r0 + RET_CHUNK, hd * RET_DK:(hd + 1) * RET_DK]
            kc = k_ref[r0:r0 + RET_CHUNK, hd * RET_DK:(hd + 1) * RET_DK]
            vc = v_ref[r0:r0 + RET_CHUNK, hd * RET_DV:(hd + 1) * RET_DV]
            qdec = qdec_ref[hd]
            kdec = kdec_ref[hd]
            s = (_dot_nt(qc, kc) * dmask_ref[hd]).astype(BF16)
            intra = _dot(s, vc)
            state = state_ref[hd]
            cross = _dot(qc, state.astype(BF16))
            cross = jnp.concatenate(
                [cross[:, t * LANES:(t + 1) * LANES] * qdec for t in range(lane_tiles_v)], axis=1)
            kf = kc.astype(F32)
            kd = jnp.concatenate(
                [kf[:, t * LANES:(t + 1) * LANES] * kdec for t in range(lane_tiles_k)], axis=1).astype(BF16)
            state_ref[hd] = state * cd_ref[hd] + _dot_tn(kd, vc)
            o = _rms(intra + cross) * gn_ref[:, hd * RET_DV:(hd + 1) * RET_DV]
            gate = gate_ref[r0:r0 + RET_CHUNK, hd * RET_DV:(hd + 1) * RET_DV].astype(F32)
            y_ref[r0:r0 + RET_CHUNK, hd * RET_DV:(hd + 1) * RET_DV] = (o * _silu(gate)).astype(BF16)


def _ret_core(q, k, v, gate, gn, batch, seq):
    n = q.shape[0]
    steps = seq // RET_ROWS
    heads = jnp.arange(RET_HEADS, dtype=F32)
    log_g = jnp.log(1.0 - 2.0 ** (-5.0 - heads))
    idx = jnp.arange(RET_CHUNK, dtype=F32)
    diff = idx[:, None] - idx[None, :]
    dmask = jnp.where(diff[None] >= 0, jnp.exp(jnp.maximum(diff, 0.0)[None] * log_g[:, None, None]), 0.0)
    q_decay = jnp.exp((idx + 1.0)[None, :] * log_g[:, None])
    k_decay = jnp.exp((RET_CHUNK - 1.0 - idx)[None, :] * log_g[:, None])
    chunk_decay = jnp.exp(RET_CHUNK * log_g)
    qdec = jnp.broadcast_to(q_decay[:, :, None], (RET_HEADS, RET_CHUNK, LANES))
    kdec = jnp.broadcast_to(k_decay[:, :, None], (RET_HEADS, RET_CHUNK, LANES))
    row_block = lambda w: pl.BlockSpec((RET_ROWS, w), lambda b, i: (b * steps + i, 0))
    table = _resident((RET_HEADS, RET_CHUNK, LANES))
    return pl.pallas_call(
        _ret_core_kernel,
        grid=(batch, steps),
        in_specs=[pl.BlockSpec(memory_space=pltpu.SMEM),
                  row_block(RET_QK_W), row_block(RET_QK_W), row_block(RET_V_W), row_block(RET_V_W),
                  _resident((1, RET_V_W)), _resident((RET_HEADS, RET_CHUNK, RET_CHUNK)), table, table],
        out_specs=row_block(RET_V_W),
        out_shape=jax.ShapeDtypeStruct((n, RET_V_W), BF16),
        scratch_shapes=[pltpu.VMEM((RET_HEADS, RET_DK, RET_DV), F32)],
        compiler_params=_params("parallel", "arbitrary"),
        name="ret_core",
    )(chunk_decay, q, k, v, gate, gn.reshape(1, RET_V_W), dmask, qdec, kdec)


def _mla_inproj_kernel(h_ref, g_ref, wlat_ref, wgate_ref, qn_ref, kvn_ref, wq_ref, wk_ref, wv_ref,
                       cos_ref, sin_ref, q_ref, k_ref, vt_ref, gate_ref):
    xn = (_rms(h_ref[...]) * g_ref[...]).astype(BF16)
    lat = _dot(xn, wlat_ref[...])
    q_lat = lat[:, :MLA_Q_RANK]
    kv_lat = lat[:, MLA_Q_RANK:MLA_Q_RANK + MLA_KV_RANK]
    k_rope = lat[:, MLA_Q_RANK + MLA_KV_RANK:]
    qn = (_rms(q_lat) * qn_ref[...]).astype(BF16)
    kvn = (_rms(kv_lat) * kvn_ref[...]).astype(BF16)
    cos = cos_ref[...]
    sin = sin_ref[...]

    def rope(x):
        swapped = pltpu.roll(x, MLA_ROPE // 2, 1) + pltpu.roll(x, LANES - MLA_ROPE // 2, 1)
        return x * cos + swapped * sin

    k_rope = rope(k_rope).astype(BF16)
    group = 4
    for g in range(MLA_HEADS // group):
        qg = _dot(qn, wq_ref[:, g * group * MLA_HEAD_PAD:(g + 1) * group * MLA_HEAD_PAD])
        kg = _dot(kvn, wk_ref[:, g * group * MLA_NOPE:(g + 1) * group * MLA_NOPE])
        for j in range(group):
            c = (g * group + j) * MLA_HEAD_PAD
            q_nope = qg[:, j * MLA_HEAD_PAD:j * MLA_HEAD_PAD + MLA_NOPE]
            q_rope = qg[:, j * MLA_HEAD_PAD + MLA_NOPE:(j + 1) * MLA_HEAD_PAD]
            q_ref[:, c:c + MLA_NOPE] = (q_nope * ATT_Q_SCALE).astype(BF16)
            q_ref[:, c + MLA_NOPE:c + MLA_HEAD_PAD] = (rope(q_rope) * ATT_Q_SCALE).astype(BF16)
            k_ref[:, c:c + MLA_NOPE] = kg[:, j * MLA_NOPE:(j + 1) * MLA_NOPE].astype(BF16)
            k_ref[:, c + MLA_NOPE:c + MLA_HEAD_PAD] = k_rope
        w = group * MLA_V
        vt_ref[g * w:(g + 1) * w, :] = _dot_nt(wv_ref[g * w:(g + 1) * w, :], kvn).astype(BF16)
        gate_ref[:, g * w:(g + 1) * w] = _dot(xn, wgate_ref[:, g * w:(g + 1) * w]).astype(BF16)


def _mla_inproj(h, gain, w_in, q_norm, w_q_b, kv_norm, w_kv_b, cos_m, sin_m):
    n = h.shape[0]
    rows = ROW_TILE
    w_lat = jnp.pad(w_in[:, :MLA_LAT_W], ((0, 0), (0, MLA_LAT_PAD - MLA_LAT_W))).astype(BF16)
    w_gate = w_in[:, MLA_LAT_W:].astype(BF16)
    w_q = jnp.pad(w_q_b.reshape(MLA_Q_RANK, MLA_HEADS, MLA_QK),
                  ((0, 0), (0, 0), (0, MLA_HEAD_PAD - MLA_QK))).reshape(MLA_Q_RANK, MLA_HEADS * MLA_HEAD_PAD)
    w_kv = w_kv_b.reshape(MLA_KV_RANK, MLA_HEADS, MLA_NOPE + MLA_V)
    w_k = w_kv[:, :, :MLA_NOPE].reshape(MLA_KV_RANK, MLA_HEADS * MLA_NOPE)
    w_vt = w_kv[:, :, MLA_NOPE:].reshape(MLA_KV_RANK, MLA_V_W).T
    qk_w = MLA_HEADS * MLA_HEAD_PAD
    row_block = lambda w: pl.BlockSpec((rows, w), lambda i: (i, 0))
    return pl.pallas_call(
        _mla_inproj_kernel,
        grid=(n // rows,),
        in_specs=[row_block(D_MODEL), _resident((1, D_MODEL)),
                  _resident((D_MODEL, MLA_LAT_PAD)), _resident((D_MODEL, MLA_V_W)),
                  _resident((1, MLA_Q_RANK)), _resident((1, MLA_KV_RANK)),
                  _resident((MLA_Q_RANK, qk_w)), _resident((MLA_KV_RANK, MLA_HEADS * MLA_NOPE)),
                  _resident((MLA_V_W, MLA_KV_RANK)),
                  row_block(LANES), row_block(LANES)],
        out_specs=[row_block(qk_w), row_block(qk_w), pl.BlockSpec((MLA_V_W, rows), lambda i: (0, i)),
                   row_block(MLA_V_W)],
        out_shape=[jax.ShapeDtypeStruct((n, qk_w), BF16), jax.ShapeDtypeStruct((n, qk_w), BF16),
                   jax.ShapeDtypeStruct((MLA_V_W, n), BF16), jax.ShapeDtypeStruct((n, MLA_V_W), BF16)],
        compiler_params=_params("parallel"),
        name="mla_inproj",
    )(h, gain.reshape(1, D_MODEL), w_lat, w_gate, q_norm.reshape(1, MLA_Q_RANK),
      kv_norm.reshape(1, MLA_KV_RANK), w_q.astype(BF16), w_k.astype(BF16), w_vt.astype(BF16), cos_m, sin_m)


def _mla_attn_kernel(q_ref, k_ref, vt_ref, gate_ref, y_ref):
    seq = q_ref.shape[0]
    tq, tk = ATT_Q_TILE, ATT_K_TILE
    key_pos = lax.broadcasted_iota(jnp.int32, (tk, tq), 0)
    query_pos = lax.broadcasted_iota(jnp.int32, (tk, tq), 1)
    blocks = [(qi, kj) for qi in range(seq // tq) for kj in range((qi + 1) * tq // tk)]

    def scores(qi, kj):
        q = q_ref[qi * tq:(qi + 1) * tq, :]
        return _dot_nt(k_ref[kj * tk:(kj + 1) * tk, :], q)

    pending = [scores(*blk) for blk in blocks[:ATT_PREFETCH]]
    m = l = acc = None
    for n, (qi, kj) in enumerate(blocks):
        s = pending.pop(0)
        if n + ATT_PREFETCH < len(blocks):
            pending.append(scores(*blocks[n + ATT_PREFETCH]))
        if (kj + 1) * tk > qi * tq:
            s = jnp.where(key_pos + (kj * tk - qi * tq) <= query_pos, s, MASK_VALUE)
        vt = vt_ref[:, kj * tk:(kj + 1) * tk]
        m_blk = jnp.max(s, axis=0, keepdims=True)
        if kj == 0:
            m = m_blk
            p = jnp.exp2(s - m)
            l = jnp.sum(p, axis=0, keepdims=True)
            acc = _dot(vt, p.astype(BF16))
        else:
            m_new = jnp.maximum(m, m_blk)
            p = jnp.exp2(s - m_new)
            alpha = jnp.exp2(m - m_new)
            l = alpha * l + jnp.sum(p, axis=0, keepdims=True)
            acc = alpha * acc + _dot(vt, p.astype(BF16))
            m = m_new
        if (kj + 1) * tk == (qi + 1) * tq:
            o = (acc * (1.0 / l)).T
            gate = gate_ref[qi * tq:(qi + 1) * tq, :].astype(F32)
            y_ref[qi * tq:(qi + 1) * tq, :] = (o * _silu(gate)).astype(BF16)


def _mla_attn(q, k, vt, gate, batch, seq):
    n = q.shape[0]
    return pl.pallas_call(
        _mla_attn_kernel,
        grid=(batch, MLA_HEADS),
        in_specs=[pl.BlockSpec((seq, MLA_HEAD_PAD), lambda b, h: (b, h)),
                  pl.BlockSpec((seq, MLA_HEAD_PAD), lambda b, h: (b, h)),
                  pl.BlockSpec((MLA_V, seq), lambda b, h: (h, b)),
                  pl.BlockSpec((seq, MLA_V), lambda b, h: (b, h))],
        out_specs=pl.BlockSpec((seq, MLA_V), lambda b, h: (b, h)),
        out_shape=jax.ShapeDtypeStruct((n, MLA_V_W), BF16),
        compiler_params=_params("parallel", "parallel"),
        name="mla_attn",
    )(q, k, vt, gate)


def _out_proj_kernel(y_ref, w_ref, h_ref, o_ref):
    o_ref[...] = h_ref[...] + _dot(y_ref[...], w_ref[...])


def _out_proj_norm_kernel(y_ref, w_ref, h_ref, g_ref, o_ref):
    o_ref[...] = _rms(h_ref[...] + _dot(y_ref[...], w_ref[...])) * g_ref[...]


def _out_proj(y, w_out, h, final_gain=None):
    n, width = y.shape
    rows = ROW_TILE
    row_block = lambda w: pl.BlockSpec((rows, w), lambda i: (i, 0))
    in_specs = [row_block(width), _resident((width, D_MODEL)), row_block(D_MODEL)]
    args = [y, w_out.astype(BF16), h]
    body = _out_proj_kernel
    if final_gain is not None:
        in_specs.append(_resident((1, D_MODEL)))
        args.append(final_gain.reshape(1, D_MODEL))
        body = _out_proj_norm_kernel
    return pl.pallas_call(
        body,
        grid=(n // rows,),
        in_specs=in_specs,
        out_specs=row_block(D_MODEL),
        out_shape=jax.ShapeDtypeStruct((n, D_MODEL), F32),
        compiler_params=_params("parallel"),
        name="out_proj",
    )(*args)


def kernel(x, positions, l0_norm, l0_ret_w_in, l0_ret_gn, l0_ret_w_out, l1_norm, l1_mla_w_in, l1_mla_q_norm, l1_mla_w_q_b, l1_mla_kv_norm, l1_mla_w_kv_b, l1_mla_w_out, l2_norm, l2_ret_w_in, l2_ret_gn, l2_ret_w_out, l3_norm, l3_mla_w_in, l3_mla_q_norm, l3_mla_w_q_b, l3_mla_kv_norm, l3_mla_w_kv_b, l3_mla_w_out, final_norm):
    batch, seq, d = x.shape
    assert d == D_MODEL and seq % ATT_Q_TILE == 0 and ATT_Q_TILE % ATT_K_TILE == 0
    assert seq % RET_ROWS == 0 and (batch * seq) % ROW_TILE == 0
    h = x.reshape(batch * seq, d)
    cos_r, sin_r, cos_m, sin_m = _rope_tables(positions)

    def retention(h, norm, w_in, gn, w_out):
        q, k, v, gate = _ret_inproj(h, norm, w_in, cos_r, sin_r)
        y = _ret_core(q, k, v, gate, gn, batch, seq)
        return _out_proj(y, w_out, h)

    def latent_attention(h, norm, w_in, q_norm, w_q_b, kv_norm, w_kv_b, w_out, final_gain=None):
        q, k, v, gate = _mla_inproj(h, norm, w_in, q_norm, w_q_b, kv_norm, w_kv_b, cos_m, sin_m)
        y = _mla_attn(q, k, v, gate, batch, seq)
        return _out_proj(y, w_out, h, final_gain)

    h = retention(h, l0_norm, l0_ret_w_in, l0_ret_gn, l0_ret_w_out)
    h = latent_attention(h, l1_norm, l1_mla_w_in, l1_mla_q_norm, l1_mla_w_q_b, l1_mla_kv_norm,
                         l1_mla_w_kv_b, l1_mla_w_out)
    h = retention(h, l2_norm, l2_ret_w_in, l2_ret_gn, l2_ret_w_out)
    h = latent_attention(h, l3_norm, l3_mla_w_in, l3_mla_q_norm, l3_mla_w_q_b, l3_mla_kv_norm,
                         l3_mla_w_kv_b, l3_mla_w_out, final_gain=final_norm)
    return h.reshape(batch, seq, d)
```

```python
import math

import jax
import jax.numpy as jnp
from jax import lax
from jax.experimental import pallas as pl
from jax.experimental.pallas import tpu as pltpu

D_MODEL = 1024
EPS = 1e-6
ROPE_BASE = 10000.0

RET_HEADS = 4
RET_DK = 256
RET_DV = 512
RET_QK_W = RET_HEADS * RET_DK
RET_V_W = RET_HEADS * RET_DV

MLA_HEADS = 16
MLA_Q_RANK = 256
MLA_KV_RANK = 128
MLA_NOPE = 128
MLA_ROPE = 64
MLA_V = 128
MLA_QK = MLA_NOPE + MLA_ROPE
MLA_V_W = MLA_HEADS * MLA_V
MLA_LAT_W = MLA_Q_RANK + MLA_KV_RANK + MLA_ROPE
MLA_LAT_PAD = 512
MLA_HEAD_PAD = 256

LANES = 128
VMEM_LIMIT = 56 * 1024 * 1024

ROW_TILE = 512
RET_ROWS = 512
RET_BLOCK = 256
ATT_Q_TILE = 512
ATT_K_TILE = 512
ATT_PREFETCH = 2
ATT_HEADS_PER_STEP = 2

BF16 = jnp.bfloat16
F32 = jnp.float32

ATT_Q_SCALE = (MLA_QK ** -0.5) * math.log2(math.e)
MASK_VALUE = -1e30


def _dot(a, b):
    return jnp.dot(a, b, preferred_element_type=F32)


def _dot_nt(a, b):
    return lax.dot_general(a, b, (((1,), (1,)), ((), ())), preferred_element_type=F32)


def _dot_tn(a, b):
    return lax.dot_general(a, b, (((0,), (0,)), ((), ())), preferred_element_type=F32)


def _rms(x):
    return x * lax.rsqrt(jnp.mean(x * x, axis=-1, keepdims=True) + EPS)


def _silu(g):
    return g * (1.0 / (1.0 + jnp.exp(-g)))


def _resident(shape):
    return pl.BlockSpec(shape, lambda *_: (0,) * len(shape), pipeline_mode=pl.Buffered(1))


def _params(*semantics):
    return pltpu.CompilerParams(dimension_semantics=semantics, vmem_limit_bytes=VMEM_LIMIT)


def _run_pipelined(stages):
    ahead = stages[0][0]()
    for i, (_, finish) in enumerate(stages):
        result = ahead
        if i + 1 < len(stages):
            ahead = stages[i + 1][0]()
        finish(result)


def _rope_table_kernel(pos_ref, inv_r_ref, inv_m_ref, sign_m_ref,
                       cos_r_ref, sin_r_ref, cos_m_ref, sin_m_ref):
    pos = pos_ref[...].astype(F32)
    ang_r = pos * inv_r_ref[...]
    cos_r_ref[...] = jnp.cos(ang_r)
    sin_r_ref[...] = jnp.sin(ang_r)
    ang_m = pos * inv_m_ref[...]
    cos_m_ref[...] = jnp.cos(ang_m)
    sin_m_ref[...] = jnp.sin(ang_m) * sign_m_ref[...]


def _rope_tables(positions):
    n = positions.size
    rows = 2048
    pos = positions.reshape(n, 1)
    half_r = RET_DK // 2
    half_m = MLA_ROPE // 2
    inv_r = (ROPE_BASE ** (-jnp.arange(half_r, dtype=F32) / half_r)).reshape(1, LANES)
    inv_m_half = ROPE_BASE ** (-jnp.arange(half_m, dtype=F32) / half_m)
    zeros = jnp.zeros((LANES - 2 * half_m,), F32)
    inv_m = jnp.concatenate([inv_m_half, inv_m_half, zeros]).reshape(1, LANES)
    sign_m = jnp.concatenate([-jnp.ones((half_m,), F32), jnp.ones((half_m,), F32), zeros]).reshape(1, LANES)
    const = pl.BlockSpec((1, LANES), lambda i: (0, 0))
    tab = pl.BlockSpec((rows, LANES), lambda i: (i, 0))
    out = jax.ShapeDtypeStruct((n, LANES), F32)
    return pl.pallas_call(
        _rope_table_kernel,
        grid=(n // rows,),
        in_specs=[pl.BlockSpec((rows, 1), lambda i: (i, 0)), const, const, const],
        out_specs=[tab, tab, tab, tab],
        out_shape=[out, out, out, out],
        compiler_params=_params("parallel"),
        name="rope_tables",
    )(pos, inv_r, inv_m, sign_m)


def _ret_inproj_kernel(h_ref, g_ref, w_ref, cos_ref, sin_ref, gn_ref, q_ref, k_ref, v_ref, gmul_ref):
    xn = (_rms(h_ref[...]) * g_ref[...]).astype(BF16)
    half = RET_DK // 2
    width = RET_DK

    def project(col0):
        return lambda: _dot(xn, w_ref[:, col0:col0 + width])

    def store_gmul(c):
        def finish(gate):
            gmul_ref[:, c:c + width] = (gn_ref[:, c:c + width] * _silu(gate)).astype(BF16)
        return finish

    def store_roped(out_ref, c, scale):
        def finish(r):
            x1, x2, cos, sin = r[:, :half], r[:, half:], cos_ref[...], sin_ref[...]
            out_ref[:, c:c + half] = ((x1 * cos - x2 * sin) * scale).astype(BF16)
            out_ref[:, c + half:c + RET_DK] = ((x2 * cos + x1 * sin) * scale).astype(BF16)
        return finish

    def store_v(c):
        def finish(v):
            v_ref[:, c:c + width] = v.astype(BF16)
        return finish

    v_cols = range(0, RET_V_W, width)
    qk_cols = range(0, RET_QK_W, width)
    stages = [(project(2 * RET_QK_W + RET_V_W + c), store_gmul(c)) for c in v_cols]
    stages += [(project(c), store_roped(q_ref, c, 1.0)) for c in qk_cols]
    stages += [(project(RET_QK_W + c), store_roped(k_ref, c, RET_DK ** -0.5)) for c in qk_cols]
    stages += [(project(2 * RET_QK_W + c), store_v(c)) for c in v_cols]
    _run_pipelined(stages)


def _ret_inproj(h, gain, w_in, gn, cos_r, sin_r):
    n = h.shape[0]
    rows = ROW_TILE
    in_w = w_in.shape[1]
    row_block = lambda w: pl.BlockSpec((rows, w), lambda i: (i, 0))
    return pl.pallas_call(
        _ret_inproj_kernel,
        grid=(n // rows,),
        in_specs=[row_block(D_MODEL), _resident((1, D_MODEL)), _resident((D_MODEL, in_w)),
                  row_block(LANES), row_block(LANES), _resident((1, RET_V_W))],
        out_specs=[row_block(RET_QK_W), row_block(RET_QK_W), row_block(RET_V_W), row_block(RET_V_W)],
        out_shape=[jax.ShapeDtypeStruct((n, RET_QK_W), BF16), jax.ShapeDtypeStruct((n, RET_QK_W), BF16),
                   jax.ShapeDtypeStruct((n, RET_V_W), BF16), jax.ShapeDtypeStruct((n, RET_V_W), BF16)],
        compiler_params=_params("parallel"),
        name="ret_inproj",
    )(h, gain.reshape(1, D_MODEL), w_in.astype(BF16), cos_r, sin_r, gn.reshape(1, RET_V_W))


def _lane_scale(x, col):
    return jnp.concatenate([x[:, t * LANES:(t + 1) * LANES] * col for t in range(x.shape[1] // LANES)], axis=1)


def _ret_core_kernel(cd_ref, q_ref, k_ref, v_ref, gmul_ref, dmask_ref, qdec_ref, kdec_ref, y_ref, state_ref):
    @pl.when(pl.program_id(1) == 0)
    def _():
        state_ref[...] = jnp.zeros_like(state_ref)

    blk = RET_BLOCK
    items = [(c * blk, hd) for c in range(RET_ROWS // blk) for hd in range(RET_HEADS)]

    def tiles(r0, hd):
        qc = q_ref[r0:r0 + blk, hd * RET_DK:(hd + 1) * RET_DK]
        kc = k_ref[r0:r0 + blk, hd * RET_DK:(hd + 1) * RET_DK]
        vc = v_ref[r0:r0 + blk, hd * RET_DV:(hd + 1) * RET_DV]
        return qc, kc, vc

    def matmuls(r0, hd):
        qc, kc, vc = tiles(r0, hd)
        s_raw = _dot_nt(qc, kc)
        cross = _dot(qc, state_ref[hd].astype(BF16))
        kd = _lane_scale(kc.astype(F32), kdec_ref[hd]).astype(BF16)
        return s_raw, cross, _dot_tn(kd, vc)

    ahead = matmuls(*items[0])
    for n, (r0, hd) in enumerate(items):
        s_raw, cross, kv = ahead
        if n + 1 < len(items):
            ahead = matmuls(*items[n + 1])
        _, _, vc = tiles(r0, hd)
        s = (s_raw * dmask_ref[hd]).astype(BF16)
        o = _dot(s, vc) + _lane_scale(cross, qdec_ref[hd])
        state_ref[hd] = state_ref[hd] * cd_ref[hd] + kv
        gmul = gmul_ref[r0:r0 + blk, hd * RET_DV:(hd + 1) * RET_DV].astype(F32)
        y_ref[r0:r0 + blk, hd * RET_DV:(hd + 1) * RET_DV] = (_rms(o) * gmul).astype(BF16)


def _ret_core(q, k, v, gmul, batch, seq):
    n = q.shape[0]
    steps = seq // RET_ROWS
    blk = RET_BLOCK
    heads = jnp.arange(RET_HEADS, dtype=F32)
    log_g = jnp.log(1.0 - 2.0 ** (-5.0 - heads))
    idx = jnp.arange(blk, dtype=F32)
    diff = idx[:, None] - idx[None, :]
    dmask = jnp.where(diff[None] >= 0, jnp.exp(jnp.maximum(diff, 0.0)[None] * log_g[:, None, None]), 0.0)
    q_decay = jnp.exp((idx + 1.0)[None, :] * log_g[:, None])
    k_decay = jnp.exp((blk - 1.0 - idx)[None, :] * log_g[:, None])
    block_decay = jnp.exp(blk * log_g)
    qdec = jnp.broadcast_to(q_decay[:, :, None], (RET_HEADS, blk, LANES))
    kdec = jnp.broadcast_to(k_decay[:, :, None], (RET_HEADS, blk, LANES))
    row_block = lambda w: pl.BlockSpec((RET_ROWS, w), lambda b, i: (b * steps + i, 0))
    table = _resident((RET_HEADS, blk, LANES))
    return pl.pallas_call(
        _ret_core_kernel,
        grid=(batch, steps),
        in_specs=[pl.BlockSpec(memory_space=pltpu.SMEM),
                  row_block(RET_QK_W), row_block(RET_QK_W), row_block(RET_V_W), row_block(RET_V_W),
                  _resident((RET_HEADS, blk, blk)), table, table],
        out_specs=row_block(RET_V_W),
        out_shape=jax.ShapeDtypeStruct((n, RET_V_W), BF16),
        scratch_shapes=[pltpu.VMEM((RET_HEADS, RET_DK, RET_DV), F32)],
        compiler_params=_params("parallel", "arbitrary"),
        name="ret_core",
    )(block_decay, q, k, v, gmul, dmask, qdec, kdec)


def _mla_inproj_kernel(h_ref, g_ref, wlat_ref, wgate_ref, qn_ref, kvn_ref, wq_ref, wk_ref, wv_ref,
                       cos_ref, sin_ref, q_ref, k_ref, vt_ref, gmul_ref):
    xn = (_rms(h_ref[...]) * g_ref[...]).astype(BF16)
    group = 4
    groups = range(MLA_HEADS // group)
    latent = {}

    def rope(x):
        swapped = pltpu.roll(x, MLA_ROPE // 2, 1) + pltpu.roll(x, LANES - MLA_ROPE // 2, 1)
        return x * cos_ref[...] + swapped * sin_ref[...]

    def finish_latent(lat):
        q_lat = lat[:, :MLA_Q_RANK]
        kv_lat = lat[:, MLA_Q_RANK:MLA_Q_RANK + MLA_KV_RANK]
        latent["qn"] = (_rms(q_lat) * qn_ref[...]).astype(BF16)
        latent["kvn"] = (_rms(kv_lat) * kvn_ref[...]).astype(BF16)
        latent["k_rope"] = rope(lat[:, MLA_Q_RANK + MLA_KV_RANK:]).astype(BF16)

    def store_gmul(g):
        w = group * MLA_V
        def finish(gate):
            gmul_ref[:, g * w:(g + 1) * w] = _silu(gate).astype(BF16)
        return lambda: _dot(xn, wgate_ref[:, g * w:(g + 1) * w]), finish

    def store_q(g):
        w = group * MLA_HEAD_PAD
        def finish(qg):
            for j in range(group):
                c = j * MLA_HEAD_PAD
                q_ref[:, g * w + c:g * w + c + MLA_NOPE] = (qg[:, c:c + MLA_NOPE] * ATT_Q_SCALE).astype(BF16)
                q_ref[:, g * w + c + MLA_NOPE:g * w + c + MLA_HEAD_PAD] = (
                    rope(qg[:, c + MLA_NOPE:c + MLA_HEAD_PAD]) * ATT_Q_SCALE).astype(BF16)
        return lambda: _dot(latent["qn"], wq_ref[:, g * w:(g + 1) * w]), finish

    def store_k(g):
        w = group * MLA_NOPE
        def finish(kg):
            for j in range(group):
                c = (g * group + j) * MLA_HEAD_PAD
                k_ref[:, c:c + MLA_NOPE] = kg[:, j * MLA_NOPE:(j + 1) * MLA_NOPE].astype(BF16)
                k_ref[:, c + MLA_NOPE:c + MLA_HEAD_PAD] = latent["k_rope"]
        return lambda: _dot(latent["kvn"], wk_ref[:, g * w:(g + 1) * w]), finish

    def store_vt(g):
        w = group * MLA_V
        def finish(vt):
            vt_ref[g * w:(g + 1) * w, :] = vt.astype(BF16)
        return lambda: _dot_nt(wv_ref[g * w:(g + 1) * w, :], latent["kvn"]), finish

    stages = [(lambda: _dot(xn, wlat_ref[...]), finish_latent)]
    stages += [store_gmul(g) for g in groups]
    stages += [store_q(g) for g in groups] + [store_k(g) for g in groups] + [store_vt(g) for g in groups]
    _run_pipelined(stages)


def _mla_inproj(h, gain, w_in, q_norm, w_q_b, kv_norm, w_kv_b, cos_m, sin_m):
    n = h.shape[0]
    rows = ROW_TILE
    w_lat = jnp.pad(w_in[:, :MLA_LAT_W], ((0, 0), (0, MLA_LAT_PAD - MLA_LAT_W))).astype(BF16)
    w_gate = w_in[:, MLA_LAT_W:].astype(BF16)
    w_q = jnp.pad(w_q_b.reshape(MLA_Q_RANK, MLA_HEADS, MLA_QK),
                  ((0, 0), (0, 0), (0, MLA_HEAD_PAD - MLA_QK))).reshape(MLA_Q_RANK, MLA_HEADS * MLA_HEAD_PAD)
    w_kv = w_kv_b.reshape(MLA_KV_RANK, MLA_HEADS, MLA_NOPE + MLA_V)
    w_k = w_kv[:, :, :MLA_NOPE].reshape(MLA_KV_RANK, MLA_HEADS * MLA_NOPE)
    w_vt = w_kv[:, :, MLA_NOPE:].reshape(MLA_KV_RANK, MLA_V_W).T
    qk_w = MLA_HEADS * MLA_HEAD_PAD
    row_block = lambda w: pl.BlockSpec((rows, w), lambda i: (i, 0))
    return pl.pallas_call(
        _mla_inproj_kernel,
        grid=(n // rows,),
        in_specs=[row_block(D_MODEL), _resident((1, D_MODEL)),
                  _resident((D_MODEL, MLA_LAT_PAD)), _resident((D_MODEL, MLA_V_W)),
                  _resident((1, MLA_Q_RANK)), _resident((1, MLA_KV_RANK)),
                  _resident((MLA_Q_RANK, qk_w)), _resident((MLA_KV_RANK, MLA_HEADS * MLA_NOPE)),
                  _resident((MLA_V_W, MLA_KV_RANK)),
                  row_block(LANES), row_block(LANES)],
        out_specs=[row_block(qk_w), row_block(qk_w), pl.BlockSpec((MLA_V_W, rows), lambda i: (0, i)),
                   row_block(MLA_V_W)],
        out_shape=[jax.ShapeDtypeStruct((n, qk_w), BF16), jax.ShapeDtypeStruct((n, qk_w), BF16),
                   jax.ShapeDtypeStruct((MLA_V_W, n), BF16), jax.ShapeDtypeStruct((n, MLA_V_W), BF16)],
        compiler_params=_params("parallel"),
        name="mla_inproj",
    )(h, gain.reshape(1, D_MODEL), w_lat, w_gate, q_norm.reshape(1, MLA_Q_RANK),
      kv_norm.reshape(1, MLA_KV_RANK), w_q.astype(BF16), w_k.astype(BF16), w_vt.astype(BF16), cos_m, sin_m)


def _mla_attn_kernel(q_ref, k_ref, vt_ref, gmul_ref, y_ref):
    seq = q_ref.shape[0]
    tq, tk, half = ATT_Q_TILE, ATT_K_TILE, ATT_Q_TILE // 2
    blocks = []
    for hd in range(ATT_HEADS_PER_STEP):
        for qi in range(seq // tq):
            blocks += [(hd, qi, kj * tk, tk, 0, tq, False) for kj in range(qi * tq // tk)]
            blocks += [(hd, qi, qi * tq, half, 0, tq, True), (hd, qi, qi * tq + half, half, half, half, True)]

    def scores(hd, qi, k0, kl, c0, cl, diag):
        cols = slice(hd * MLA_HEAD_PAD, (hd + 1) * MLA_HEAD_PAD)
        q = q_ref[qi * tq + c0:qi * tq + c0 + cl, cols]
        return _dot_nt(k_ref[k0:k0 + kl, cols], q)

    pending = [scores(*blk) for blk in blocks[:ATT_PREFETCH]]
    m = l = acc = None
    for n, (hd, qi, k0, kl, c0, cl, diag) in enumerate(blocks):
        s = pending.pop(0)
        if n + ATT_PREFETCH < len(blocks):
            pending.append(scores(*blocks[n + ATT_PREFETCH]))
        if diag:
            key_pos = lax.broadcasted_iota(jnp.int32, (kl, cl), 0) + (k0 - qi * tq - c0)
            s = jnp.where(key_pos <= lax.broadcasted_iota(jnp.int32, (kl, cl), 1), s, MASK_VALUE)
        vt = vt_ref[hd * MLA_V:(hd + 1) * MLA_V, k0:k0 + kl]
        m_blk = jnp.max(s, axis=0, keepdims=True)
        if k0 == 0:
            m = m_blk
            p = jnp.exp2(s - m)
            l = jnp.sum(p, axis=0, keepdims=True)
            acc = _dot(vt, p.astype(BF16))
        else:
            m_old, l_old, acc_old = m[:, c0:], l[:, c0:], acc[:, c0:]
            m_new = jnp.maximum(m_old, m_blk)
            p = jnp.exp2(s - m_new)
            alpha = jnp.exp2(m_old - m_new)
            l_new = alpha * l_old + jnp.sum(p, axis=0, keepdims=True)
            acc_new = alpha * acc_old + _dot(vt, p.astype(BF16))
            if c0:
                m_new = jnp.concatenate([m[:, :c0], m_new], axis=1)
                l_new = jnp.concatenate([l[:, :c0], l_new], axis=1)
                acc_new = jnp.concatenate([acc[:, :c0], acc_new], axis=1)
            m, l, acc = m_new, l_new, acc_new
        if k0 + kl == (qi + 1) * tq:
            o = (acc * (1.0 / l)).T
            gmul = gmul_ref[qi * tq:(qi + 1) * tq, hd * MLA_V:(hd + 1) * MLA_V].astype(F32)
            y_ref[qi * tq:(qi + 1) * tq, hd * MLA_V:(hd + 1) * MLA_V] = (o * gmul).astype(BF16)


def _mla_attn(q, k, vt, gmul, batch, seq):
    n = q.shape[0]
    hps = ATT_HEADS_PER_STEP
    return pl.pallas_call(
        _mla_attn_kernel,
        grid=(batch, MLA_HEADS // hps),
        in_specs=[pl.BlockSpec((seq, hps * MLA_HEAD_PAD), lambda b, h: (b, h)),
                  pl.BlockSpec((seq, hps * MLA_HEAD_PAD), lambda b, h: (b, h)),
                  pl.BlockSpec((hps * MLA_V, seq), lambda b, h: (h, b)),
                  pl.BlockSpec((seq, hps * MLA_V), lambda b, h: (b, h))],
        out_specs=pl.BlockSpec((seq, hps * MLA_V), lambda b, h: (b, h)),
        out_shape=jax.ShapeDtypeStruct((n, MLA_V_W), BF16),
        compiler_params=_params("parallel", "parallel"),
        name="mla_attn",
    )(q, k, vt, gmul)


def _out_proj_kernel(y_ref, w_ref, h_ref, o_ref):
    o_ref[...] = h_ref[...] + _dot(y_ref[...], w_ref[...])


def _out_proj_norm_kernel(y_ref, w_ref, h_ref, g_ref, o_ref):
    o_ref[...] = _rms(h_ref[...] + _dot(y_ref[...], w_ref[...])) * g_ref[...]


def _out_proj(y, w_out, h, final_gain=None):
    n, width = y.shape
    rows = ROW_TILE
    row_block = lambda w: pl.BlockSpec((rows, w), lambda i: (i, 0))
    in_specs = [row_block(width), _resident((width, D_MODEL)), row_block(D_MODEL)]
    args = [y, w_out.astype(BF16), h]
    body = _out_proj_kernel
    if final_gain is not None:
        in_specs.append(_resident((1, D_MODEL)))
        args.append(final_gain.reshape(1, D_MODEL))
        body = _out_proj_norm_kernel
    return pl.pallas_call(
        body,
        grid=(n // rows,),
        in_specs=in_specs,
        out_specs=row_block(D_MODEL),
        out_shape=jax.ShapeDtypeStruct((n, D_MODEL), F32),
        compiler_params=_params("parallel"),
        name="out_proj",
    )(*args)


def kernel(x, positions, l0_norm, l0_ret_w_in, l0_ret_gn, l0_ret_w_out, l1_norm, l1_mla_w_in, l1_mla_q_norm, l1_mla_w_q_b, l1_mla_kv_norm, l1_mla_w_kv_b, l1_mla_w_out, l2_norm, l2_ret_w_in, l2_ret_gn, l2_ret_w_out, l3_norm, l3_mla_w_in, l3_mla_q_norm, l3_mla_w_q_b, l3_mla_kv_norm, l3_mla_w_kv_b, l3_mla_w_out, final_norm):
    batch, seq, d = x.shape
    assert d == D_MODEL and seq % ATT_Q_TILE == 0 and ATT_Q_TILE % ATT_K_TILE == 0
    assert seq % RET_ROWS == 0 and RET_ROWS % RET_BLOCK == 0 and RET_HEADS > 1 and (batch * seq) % ROW_TILE == 0
    h = x.reshape(batch * seq, d)
    cos_r, sin_r, cos_m, sin_m = _rope_tables(positions)

    def retention(h, norm, w_in, gn, w_out):
        q, k, v, gmul = _ret_inproj(h, norm, w_in, gn, cos_r, sin_r)
        y = _ret_core(q, k, v, gmul, batch, seq)
        return _out_proj(y, w_out, h)

    def latent_attention(h, norm, w_in, q_norm, w_q_b, kv_norm, w_kv_b, w_out, final_gain=None):
        q, k, vt, gmul = _mla_inproj(h, norm, w_in, q_norm, w_q_b, kv_norm, w_kv_b, cos_m, sin_m)
        y = _mla_attn(q, k, vt, gmul, batch, seq)
        return _out_proj(y, w_out, h, final_gain)

    h = retention(h, l0_norm, l0_ret_w_in, l0_ret_gn, l0_ret_w_out)
    h = latent_attention(h, l1_norm, l1_mla_w_in, l1_mla_q_norm, l1_mla_w_q_b, l1_mla_kv_norm,
                         l1_mla_w_kv_b, l1_mla_w_out)
    h = retention(h, l2_norm, l2_ret_w_in, l2_ret_gn, l2_ret_w_out)
    h = latent_attention(h, l3_norm, l3_mla_w_in, l3_mla_q_norm, l3_mla_w_q_b, l3_mla_kv_norm,
                         l3_mla_w_kv_b, l3_mla_w_out, final_gain=final_norm)
    return h.reshape(batch, seq, d)
```

```python
import functools
import math

import jax
import jax.numpy as jnp
from jax import lax
from jax.experimental import pallas as pl
from jax.experimental.pallas import tpu as pltpu

D_MODEL = 1024
EPS = 1e-6
ROPE_BASE = 10000.0

RET_HEADS = 4
RET_DK = 256
RET_DV = 512
RET_QK_W = RET_HEADS * RET_DK
RET_V_W = RET_HEADS * RET_DV

MLA_HEADS = 16
MLA_Q_RANK = 256
MLA_KV_RANK = 128
MLA_NOPE = 128
MLA_ROPE = 64
MLA_V = 128
MLA_QK = MLA_NOPE + MLA_ROPE
MLA_V_W = MLA_HEADS * MLA_V
MLA_LAT_W = MLA_Q_RANK + MLA_KV_RANK + MLA_ROPE
MLA_LAT_PAD = 512
MLA_HEAD_PAD = 256

LANES = 128
VMEM_LIMIT = 56 * 1024 * 1024

ROW_TILE = 512
RET_ROWS = 1024
RET_BLOCK = 256
ATT_Q_TILE = 512
ATT_K_TILE = 512
ATT_PREFETCH = 2
ATT_HEADS_PER_STEP = 2

BF16 = jnp.bfloat16
F32 = jnp.float32

ATT_Q_SCALE = (MLA_QK ** -0.5) * math.log2(math.e)
MASK_VALUE = -1e30


def _dot(a, b):
    return jnp.dot(a, b, preferred_element_type=F32)


def _dot_nt(a, b):
    return lax.dot_general(a, b, (((1,), (1,)), ((), ())), preferred_element_type=F32)


def _dot_tn(a, b):
    return lax.dot_general(a, b, (((0,), (0,)), ((), ())), preferred_element_type=F32)


def _rms(x):
    return x * lax.rsqrt(jnp.mean(x * x, axis=-1, keepdims=True) + EPS)


def _silu(g):
    half = 0.5 * g
    return half + half * jnp.tanh(half)


def _resident(shape):
    return pl.BlockSpec(shape, lambda *_: (0,) * len(shape), pipeline_mode=pl.Buffered(1))


def _params(*semantics):
    return pltpu.CompilerParams(dimension_semantics=semantics, vmem_limit_bytes=VMEM_LIMIT)


def _run_pipelined(stages):
    ahead = stages[0][0]()
    for i, (_, finish) in enumerate(stages):
        result = ahead
        if i + 1 < len(stages):
            ahead = stages[i + 1][0]()
        finish(result)


def _rope_table_kernel(pos_ref, inv_r_ref, sel_cos_ref, sel_sin_ref,
                       cos_r_ref, sin_r_ref, cos_m_ref, sin_m_ref):
    pos = pos_ref[...].astype(F32)
    ang = pos * inv_r_ref[...]
    cos = jnp.cos(ang)
    sin = jnp.sin(ang)
    cos_r_ref[...] = cos
    sin_r_ref[...] = sin
    pick = lambda x, sel_ref: jnp.dot(x, sel_ref[...], precision=lax.Precision.HIGHEST,
                                      preferred_element_type=F32)
    cos_m_ref[...] = pick(cos, sel_cos_ref)
    sin_m_ref[...] = pick(sin, sel_sin_ref)


def _rope_tables(positions):
    n = positions.size
    rows = 2048
    pos = positions.reshape(n, 1)
    half_r = RET_DK // 2
    half_m = MLA_ROPE // 2
    stride = half_r // half_m
    assert half_r == LANES and half_r % half_m == 0
    inv_r = (ROPE_BASE ** (-jnp.arange(half_r, dtype=F32) / half_r)).reshape(1, LANES)
    src = stride * jnp.arange(half_m)
    sel_cos = jnp.zeros((LANES, LANES), F32).at[src, jnp.arange(half_m)].set(1.0)
    sel_cos = sel_cos.at[src, half_m + jnp.arange(half_m)].set(1.0)
    sel_sin = jnp.zeros((LANES, LANES), F32).at[src, jnp.arange(half_m)].set(-1.0)
    sel_sin = sel_sin.at[src, half_m + jnp.arange(half_m)].set(1.0)
    const = pl.BlockSpec((1, LANES), lambda i: (0, 0))
    sel = pl.BlockSpec((LANES, LANES), lambda i: (0, 0))
    tab = pl.BlockSpec((rows, LANES), lambda i: (i, 0))
    out = jax.ShapeDtypeStruct((n, LANES), F32)
    return pl.pallas_call(
        _rope_table_kernel,
        grid=(n // rows,),
        in_specs=[pl.BlockSpec((rows, 1), lambda i: (i, 0)), const, sel, sel],
        out_specs=[tab, tab, tab, tab],
        out_shape=[out, out, out, out],
        compiler_params=_params("parallel"),
        name="rope_tables",
    )(pos, inv_r, sel_cos, sel_sin)


def _residual_stream(prev_refs, h_ref, hout_ref):
    if not prev_refs:
        return h_ref[...]
    y_ref, wout_ref = prev_refs
    h = h_ref[...] + _dot(y_ref[...], wout_ref[...])
    hout_ref[...] = h
    return h


def _split_refs(fused, refs, n_in):
    refs = list(refs)
    prev = (refs.pop(0), refs.pop(0)) if fused else ()
    ins, outs = refs[:n_in], refs[n_in:]
    hout_ref = outs.pop(0) if fused else None
    return prev, hout_ref, ins, outs


def _ret_inproj_kernel(fused, *refs):
    prev, hout_ref, (h_ref, g_ref, w_ref, cos_ref, sin_ref, gn_ref), (q_ref, k_ref, v_ref, gmul_ref) = (
        _split_refs(fused, refs, 6))
    xn = (_rms(_residual_stream(prev, h_ref, hout_ref)) * g_ref[...]).astype(BF16)
    half = RET_DK // 2
    width = RET_DK

    def project(col0):
        return lambda: _dot(xn, w_ref[:, col0:col0 + width])

    def store_gmul(c):
        def finish(gate):
            gmul_ref[:, c:c + width] = (gn_ref[:, c:c + width] * _silu(gate)).astype(BF16)
        return finish

    def store_roped(out_ref, c, scale):
        def finish(r):
            x1, x2, cos, sin = r[:, :half], r[:, half:], cos_ref[...], sin_ref[...]
            out_ref[:, c:c + half] = ((x1 * cos - x2 * sin) * scale).astype(BF16)
            out_ref[:, c + half:c + RET_DK] = ((x2 * cos + x1 * sin) * scale).astype(BF16)
        return finish

    def store_v(c):
        def finish(v):
            v_ref[:, c:c + width] = v.astype(BF16)
        return finish

    v_cols = range(0, RET_V_W, width)
    qk_cols = range(0, RET_QK_W, width)
    stages = [(project(2 * RET_QK_W + RET_V_W + c), store_gmul(c)) for c in v_cols]
    stages += [(project(c), store_roped(q_ref, c, 1.0)) for c in qk_cols]
    stages += [(project(RET_QK_W + c), store_roped(k_ref, c, RET_DK ** -0.5)) for c in qk_cols]
    stages += [(project(2 * RET_QK_W + c), store_v(c)) for c in v_cols]
    _run_pipelined(stages)


def _prev_specs(prev, rows):
    if prev is None:
        return [], [], [], []
    y, w_out = prev
    n, width = y.shape
    row_block = lambda w: pl.BlockSpec((rows, w), lambda i: (i, 0))
    return ([y, w_out.astype(BF16)], [row_block(width), _resident((width, D_MODEL))],
            [row_block(D_MODEL)], [jax.ShapeDtypeStruct((n, D_MODEL), F32)])


def _ret_inproj(h, gain, w_in, gn, cos_r, sin_r, prev=None):
    n = h.shape[0]
    rows = ROW_TILE
    in_w = w_in.shape[1]
    row_block = lambda w: pl.BlockSpec((rows, w), lambda i: (i, 0))
    prev_args, prev_in_specs, prev_out_specs, prev_out_shape = _prev_specs(prev, rows)
    outs = pl.pallas_call(
        functools.partial(_ret_inproj_kernel, prev is not None),
        grid=(n // rows,),
        in_specs=prev_in_specs + [row_block(D_MODEL), _resident((1, D_MODEL)), _resident((D_MODEL, in_w)),
                                  row_block(LANES), row_block(LANES), _resident((1, RET_V_W))],
        out_specs=prev_out_specs + [row_block(RET_QK_W), row_block(RET_QK_W), row_block(RET_V_W),
                                    row_block(RET_V_W)],
        out_shape=prev_out_shape + [jax.ShapeDtypeStruct((n, RET_QK_W), BF16),
                                    jax.ShapeDtypeStruct((n, RET_QK_W), BF16),
                                    jax.ShapeDtypeStruct((n, RET_V_W), BF16),
                                    jax.ShapeDtypeStruct((n, RET_V_W), BF16)],
        compiler_params=_params("parallel"),
        name="ret_inproj",
    )(*prev_args, h, gain.reshape(1, D_MODEL), w_in.astype(BF16), cos_r, sin_r, gn.reshape(1, RET_V_W))
    return ([h] if prev is None else []) + list(outs)


def _lane_scale(x, col):
    return jnp.concatenate([x[:, t * LANES:(t + 1) * LANES] * col for t in range(x.shape[1] // LANES)], axis=1)


def _ret_core_kernel(cd_ref, q_ref, k_ref, v_ref, gmul_ref, dmask_ref, qdec_ref, kdec_ref, y_ref, state_ref):
    @pl.when(pl.program_id(1) == 0)
    def _():
        state_ref[...] = jnp.zeros_like(state_ref)

    blk = RET_BLOCK
    items = [(c * blk, hd) for c in range(RET_ROWS // blk) for hd in range(RET_HEADS)]

    def tiles(r0, hd):
        qc = q_ref[r0:r0 + blk, hd * RET_DK:(hd + 1) * RET_DK]
        kc = k_ref[r0:r0 + blk, hd * RET_DK:(hd + 1) * RET_DK]
        vc = v_ref[r0:r0 + blk, hd * RET_DV:(hd + 1) * RET_DV]
        return qc, kc, vc

    def matmuls(r0, hd):
        qc, kc, vc = tiles(r0, hd)
        s_raw = _dot_nt(qc, kc)
        cross = _dot(qc, state_ref[hd].astype(BF16))
        kd = _lane_scale(kc.astype(F32), kdec_ref[hd]).astype(BF16)
        return s_raw, cross, _dot_tn(kd, vc)

    ahead = matmuls(*items[0])
    for n, (r0, hd) in enumerate(items):
        s_raw, cross, kv = ahead
        if n + 1 < len(items):
            ahead = matmuls(*items[n + 1])
        _, _, vc = tiles(r0, hd)
        s = (s_raw * dmask_ref[hd]).astype(BF16)
        o = _dot(s, vc) + _lane_scale(cross, qdec_ref[hd])
        state_ref[hd] = state_ref[hd] * cd_ref[hd] + kv
        gmul = gmul_ref[r0:r0 + blk, hd * RET_DV:(hd + 1) * RET_DV].astype(F32)
        y_ref[r0:r0 + blk, hd * RET_DV:(hd + 1) * RET_DV] = (_rms(o) * gmul).astype(BF16)


def _ret_core(q, k, v, gmul, batch, seq):
    n = q.shape[0]
    steps = seq // RET_ROWS
    blk = RET_BLOCK
    heads = jnp.arange(RET_HEADS, dtype=F32)
    log_g = jnp.log(1.0 - 2.0 ** (-5.0 - heads))
    idx = jnp.arange(blk, dtype=F32)
    diff = idx[:, None] - idx[None, :]
    dmask = jnp.where(diff[None] >= 0, jnp.exp(jnp.maximum(diff, 0.0)[None] * log_g[:, None, None]), 0.0)
    q_decay = jnp.exp((idx + 1.0)[None, :] * log_g[:, None])
    k_decay = jnp.exp((blk - 1.0 - idx)[None, :] * log_g[:, None])
    block_decay = jnp.exp(blk * log_g)
    qdec = jnp.broadcast_to(q_decay[:, :, None], (RET_HEADS, blk, LANES))
    kdec = jnp.broadcast_to(k_decay[:, :, None], (RET_HEADS, blk, LANES))
    row_block = lambda w: pl.BlockSpec((RET_ROWS, w), lambda b, i: (b * steps + i, 0))
    table = _resident((RET_HEADS, blk, LANES))
    return pl.pallas_call(
        _ret_core_kernel,
        grid=(batch, steps),
        in_specs=[pl.BlockSpec(memory_space=pltpu.SMEM),
                  row_block(RET_QK_W), row_block(RET_QK_W), row_block(RET_V_W), row_block(RET_V_W),
                  _resident((RET_HEADS, blk, blk)), table, table],
        out_specs=row_block(RET_V_W),
        out_shape=jax.ShapeDtypeStruct((n, RET_V_W), BF16),
        scratch_shapes=[pltpu.VMEM((RET_HEADS, RET_DK, RET_DV), F32)],
        compiler_params=_params("parallel", "arbitrary"),
        name="ret_core",
    )(block_decay, q, k, v, gmul, dmask, qdec, kdec)


def _mla_inproj_kernel(y_ref, wout_ref, h_ref, g_ref, wlat_ref, wgate_ref, qn_ref, kvn_ref, wq_ref, wk_ref, wv_ref,
                       cos_ref, sin_ref, hout_ref, q_ref, k_ref, vt_ref, gmul_ref):
    xn = (_rms(_residual_stream((y_ref, wout_ref), h_ref, hout_ref)) * g_ref[...]).astype(BF16)
    group = 4
    groups = range(MLA_HEADS // group)
    latent = {}

    def rope(x):
        swapped = pltpu.roll(x, MLA_ROPE // 2, 1) + pltpu.roll(x, LANES - MLA_ROPE // 2, 1)
        return x * cos_ref[...] + swapped * sin_ref[...]

    def finish_latent(lat):
        q_lat = lat[:, :MLA_Q_RANK]
        kv_lat = lat[:, MLA_Q_RANK:MLA_Q_RANK + MLA_KV_RANK]
        latent["qn"] = (_rms(q_lat) * qn_ref[...]).astype(BF16)
        latent["kvn"] = (_rms(kv_lat) * kvn_ref[...]).astype(BF16)
        latent["k_rope"] = rope(lat[:, MLA_Q_RANK + MLA_KV_RANK:]).astype(BF16)

    def store_gmul(g):
        w = group * MLA_V
        def finish(gate):
            gmul_ref[:, g * w:(g + 1) * w] = _silu(gate).astype(BF16)
        return lambda: _dot(xn, wgate_ref[:, g * w:(g + 1) * w]), finish

    def store_q(g):
        w = group * MLA_HEAD_PAD
        def finish(qg):
            for j in range(group):
                c = j * MLA_HEAD_PAD
                q_ref[:, g * w + c:g * w + c + MLA_NOPE] = (qg[:, c:c + MLA_NOPE] * ATT_Q_SCALE).astype(BF16)
                q_ref[:, g * w + c + MLA_NOPE:g * w + c + MLA_HEAD_PAD] = (
                    rope(qg[:, c + MLA_NOPE:c + MLA_HEAD_PAD]) * ATT_Q_SCALE).astype(BF16)
        return lambda: _dot(latent["qn"], wq_ref[:, g * w:(g + 1) * w]), finish

    def store_k(g):
        w = group * MLA_NOPE
        def finish(kg):
            for j in range(group):
                c = (g * group + j) * MLA_HEAD_PAD
                k_ref[:, c:c + MLA_NOPE] = kg[:, j * MLA_NOPE:(j + 1) * MLA_NOPE].astype(BF16)
                k_ref[:, c + MLA_NOPE:c + MLA_HEAD_PAD] = latent["k_rope"]
        return lambda: _dot(latent["kvn"], wk_ref[:, g * w:(g + 1) * w]), finish

    def store_vt(g):
        w = group * MLA_V
        def finish(vt):
            vt_ref[g * w:(g + 1) * w, :] = vt.astype(BF16)
        return lambda: _dot_nt(wv_ref[g * w:(g + 1) * w, :], latent["kvn"]), finish

    stages = [(lambda: _dot(xn, wlat_ref[...]), finish_latent)]
    stages += [store_gmul(g) for g in groups]
    stages += [store_q(g) for g in groups] + [store_k(g) for g in groups] + [store_vt(g) for g in groups]
    _run_pipelined(stages)


def _mla_inproj(h, prev, gain, w_in, q_norm, w_q_b, kv_norm, w_kv_b, cos_m, sin_m):
    n = h.shape[0]
    rows = ROW_TILE
    prev_args, prev_in_specs, prev_out_specs, prev_out_shape = _prev_specs(prev, rows)
    w_lat = jnp.pad(w_in[:, :MLA_LAT_W], ((0, 0), (0, MLA_LAT_PAD - MLA_LAT_W))).astype(BF16)
    w_gate = w_in[:, MLA_LAT_W:].astype(BF16)
    w_q = jnp.pad(w_q_b.reshape(MLA_Q_RANK, MLA_HEADS, MLA_QK),
                  ((0, 0), (0, 0), (0, MLA_HEAD_PAD - MLA_QK))).reshape(MLA_Q_RANK, MLA_HEADS * MLA_HEAD_PAD)
    w_kv = w_kv_b.reshape(MLA_KV_RANK, MLA_HEADS, MLA_NOPE + MLA_V)
    w_k = w_kv[:, :, :MLA_NOPE].reshape(MLA_KV_RANK, MLA_HEADS * MLA_NOPE)
    w_vt = w_kv[:, :, MLA_NOPE:].reshape(MLA_KV_RANK, MLA_V_W).T
    qk_w = MLA_HEADS * MLA_HEAD_PAD
    row_block = lambda w: pl.BlockSpec((rows, w), lambda i: (i, 0))
    return pl.pallas_call(
        _mla_inproj_kernel,
        grid=(n // rows,),
        in_specs=prev_in_specs + [
            row_block(D_MODEL), _resident((1, D_MODEL)),
            _resident((D_MODEL, MLA_LAT_PAD)), _resident((D_MODEL, MLA_V_W)),
            _resident((1, MLA_Q_RANK)), _resident((1, MLA_KV_RANK)),
            _resident((MLA_Q_RANK, qk_w)), _resident((MLA_KV_RANK, MLA_HEADS * MLA_NOPE)),
            _resident((MLA_V_W, MLA_KV_RANK)),
            row_block(LANES), row_block(LANES)],
        out_specs=prev_out_specs + [row_block(qk_w), row_block(qk_w),
                                    pl.BlockSpec((MLA_V_W, rows), lambda i: (0, i)), row_block(MLA_V_W)],
        out_shape=prev_out_shape + [jax.ShapeDtypeStruct((n, qk_w), BF16), jax.ShapeDtypeStruct((n, qk_w), BF16),
                                    jax.ShapeDtypeStruct((MLA_V_W, n), BF16),
                                    jax.ShapeDtypeStruct((n, MLA_V_W), BF16)],
        compiler_params=_params("parallel"),
        name="mla_inproj",
    )(*prev_args, h, gain.reshape(1, D_MODEL), w_lat, w_gate, q_norm.reshape(1, MLA_Q_RANK),
      kv_norm.reshape(1, MLA_KV_RANK), w_q.astype(BF16), w_k.astype(BF16), w_vt.astype(BF16), cos_m, sin_m)


def _mla_attn_kernel(q_ref, k_ref, vt_ref, gmul_ref, y_ref):
    seq = q_ref.shape[0]
    tq, tk, half = ATT_Q_TILE, ATT_K_TILE, ATT_Q_TILE // 2
    blocks = []
    for hd in range(ATT_HEADS_PER_STEP):
        for qi in range(seq // tq):
            blocks += [(hd, qi, kj * tk, tk, 0, tq, False) for kj in range(qi * tq // tk)]
            blocks += [(hd, qi, qi * tq, half, 0, tq, True), (hd, qi, qi * tq + half, half, half, half, True)]

    def scores(hd, qi, k0, kl, c0, cl, diag):
        cols = slice(hd * MLA_HEAD_PAD, (hd + 1) * MLA_HEAD_PAD)
        q = q_ref[qi * tq + c0:qi * tq + c0 + cl, cols]
        return _dot_nt(k_ref[k0:k0 + kl, cols], q)

    pending = [scores(*blk) for blk in blocks[:ATT_PREFETCH]]
    m = l = acc = None
    for n, (hd, qi, k0, kl, c0, cl, diag) in enumerate(blocks):
        s = pending.pop(0)
        if n + ATT_PREFETCH < len(blocks):
            pending.append(scores(*blocks[n + ATT_PREFETCH]))
        if diag:
            key_pos = lax.broadcasted_iota(jnp.int32, (kl, cl), 0) + (k0 - qi * tq - c0)
            s = jnp.where(key_pos <= lax.broadcasted_iota(jnp.int32, (kl, cl), 1), s, MASK_VALUE)
        vt = vt_ref[hd * MLA_V:(hd + 1) * MLA_V, k0:k0 + kl]
        m_blk = jnp.max(s, axis=0, keepdims=True)
        if k0 == 0:
            m = m_blk
            p = jnp.exp2(s - m)
            l = jnp.sum(p, axis=0, keepdims=True)
            acc = _dot(vt, p.astype(BF16))
        else:
            m_old, l_old, acc_old = m[:, c0:], l[:, c0:], acc[:, c0:]
            m_new = jnp.maximum(m_old, m_blk)
            p = jnp.exp2(s - m_new)
            alpha = jnp.exp2(m_old - m_new)
            l_new = alpha * l_old + jnp.sum(p, axis=0, keepdims=True)
            acc_new = alpha * acc_old + _dot(vt, p.astype(BF16))
            if c0:
                m_new = jnp.concatenate([m[:, :c0], m_new], axis=1)
                l_new = jnp.concatenate([l[:, :c0], l_new], axis=1)
                acc_new = jnp.concatenate([acc[:, :c0], acc_new], axis=1)
            m, l, acc = m_new, l_new, acc_new
        if k0 + kl == (qi + 1) * tq:
            o = (acc * (1.0 / l)).T
            gmul = gmul_ref[qi * tq:(qi + 1) * tq, hd * MLA_V:(hd + 1) * MLA_V].astype(F32)
            y_ref[qi * tq:(qi + 1) * tq, hd * MLA_V:(hd + 1) * MLA_V] = (o * gmul).astype(BF16)


def _mla_attn(q, k, vt, gmul, batch, seq):
    n = q.shape[0]
    hps = ATT_HEADS_PER_STEP
    return pl.pallas_call(
        _mla_attn_kernel,
        grid=(batch, MLA_HEADS // hps),
        in_specs=[pl.BlockSpec((seq, hps * MLA_HEAD_PAD), lambda b, h: (b, h)),
                  pl.BlockSpec((seq, hps * MLA_HEAD_PAD), lambda b, h: (b, h)),
                  pl.BlockSpec((hps * MLA_V, seq), lambda b, h: (h, b)),
                  pl.BlockSpec((seq, hps * MLA_V), lambda b, h: (b, h))],
        out_specs=pl.BlockSpec((seq, hps * MLA_V), lambda b, h: (b, h)),
        out_shape=jax.ShapeDtypeStruct((n, MLA_V_W), BF16),
        compiler_params=_params("parallel", "parallel"),
        name="mla_attn",
    )(q, k, vt, gmul)


def _final_proj_kernel(y_ref, w_ref, h_ref, g_ref, o_ref):
    o_ref[...] = _rms(h_ref[...] + _dot(y_ref[...], w_ref[...])) * g_ref[...]


def _final_proj(y, w_out, h, final_gain):
    n, width = y.shape
    rows = ROW_TILE
    row_block = lambda w: pl.BlockSpec((rows, w), lambda i: (i, 0))
    return pl.pallas_call(
        _final_proj_kernel,
        grid=(n // rows,),
        in_specs=[row_block(width), _resident((width, D_MODEL)), row_block(D_MODEL), _resident((1, D_MODEL))],
        out_specs=row_block(D_MODEL),
        out_shape=jax.ShapeDtypeStruct((n, D_MODEL), F32),
        compiler_params=_params("parallel"),
        name="final_proj",
    )(y, w_out.astype(BF16), h, final_gain.reshape(1, D_MODEL))


def kernel(x, positions, l0_norm, l0_ret_w_in, l0_ret_gn, l0_ret_w_out, l1_norm, l1_mla_w_in, l1_mla_q_norm, l1_mla_w_q_b, l1_mla_kv_norm, l1_mla_w_kv_b, l1_mla_w_out, l2_norm, l2_ret_w_in, l2_ret_gn, l2_ret_w_out, l3_norm, l3_mla_w_in, l3_mla_q_norm, l3_mla_w_q_b, l3_mla_kv_norm, l3_mla_w_kv_b, l3_mla_w_out, final_norm):
    batch, seq, d = x.shape
    assert d == D_MODEL and seq % ATT_Q_TILE == 0 and ATT_Q_TILE % ATT_K_TILE == 0
    assert seq % RET_ROWS == 0 and RET_ROWS % RET_BLOCK == 0 and RET_HEADS > 1 and (batch * seq) % ROW_TILE == 0
    h = x.reshape(batch * seq, d)
    cos_r, sin_r, cos_m, sin_m = _rope_tables(positions)

    def retention(h, prev, norm, w_in, gn):
        h, q, k, v, gmul = _ret_inproj(h, norm, w_in, gn, cos_r, sin_r, prev)
        return h, _ret_core(q, k, v, gmul, batch, seq)

    def latent_attention(h, prev, norm, w_in, q_norm, w_q_b, kv_norm, w_kv_b):
        h, q, k, vt, gmul = _mla_inproj(h, prev, norm, w_in, q_norm, w_q_b, kv_norm, w_kv_b, cos_m, sin_m)
        return h, _mla_attn(q, k, vt, gmul, batch, seq)

    h, y = retention(h, None, l0_norm, l0_ret_w_in, l0_ret_gn)
    h, y = latent_attention(h, (y, l0_ret_w_out), l1_norm, l1_mla_w_in, l1_mla_q_norm, l1_mla_w_q_b,
                            l1_mla_kv_norm, l1_mla_w_kv_b)
    h, y = retention(h, (y, l1_mla_w_out), l2_norm, l2_ret_w_in, l2_ret_gn)
    h, y = latent_attention(h, (y, l2_ret_w_out), l3_norm, l3_mla_w_in, l3_mla_q_norm, l3_mla_w_q_b,
                            l3_mla_kv_norm, l3_mla_w_kv_b)
    h = _final_proj(y, l3_mla_w_out, h, final_norm)
    return h.reshape(batch, seq, d)
```

```python
import functools
import math

import jax
import jax.numpy as jnp
from jax import lax
from jax.experimental import pallas as pl
from jax.experimental.pallas import tpu as pltpu

D_MODEL = 1024
EPS = 1e-6
ROPE_BASE = 10000.0

RET_HEADS = 4
RET_DK = 256
RET_DV = 512
RET_QK_W = RET_HEADS * RET_DK
RET_V_W = RET_HEADS * RET_DV

MLA_HEADS = 16
MLA_Q_RANK = 256
MLA_KV_RANK = 128
MLA_NOPE = 128
MLA_ROPE = 64
MLA_V = 128
MLA_QK = MLA_NOPE + MLA_ROPE
MLA_V_W = MLA_HEADS * MLA_V
MLA_LAT_W = MLA_Q_RANK + MLA_KV_RANK + MLA_ROPE
MLA_LAT_PAD = 512
MLA_HEAD_PAD = 256

LANES = 128
VMEM_LIMIT = 56 * 1024 * 1024

ROW_TILE = 512
RET_ROWS = 512
RET_BLOCK = 256
ATT_Q_TILE = 512
ATT_K_TILE = 512
ATT_PREFETCH = 3
ATT_HEADS_PER_STEP = 2

BF16 = jnp.bfloat16
F32 = jnp.float32

ATT_Q_SCALE = (MLA_QK ** -0.5) * math.log2(math.e)
MASK_VALUE = -1e30


def _dot(a, b):
    return jnp.dot(a, b, preferred_element_type=F32)


def _dot_nt(a, b):
    return lax.dot_general(a, b, (((1,), (1,)), ((), ())), preferred_element_type=F32)


def _dot_tn(a, b):
    return lax.dot_general(a, b, (((0,), (0,)), ((), ())), preferred_element_type=F32)


def _rms(x):
    return x * lax.rsqrt(jnp.mean(x * x, axis=-1, keepdims=True) + EPS)


def _silu(g):
    half = 0.5 * g
    return half + half * jnp.tanh(half)


def _resident(shape):
    return pl.BlockSpec(shape, lambda *_: (0,) * len(shape), pipeline_mode=pl.Buffered(1))


def _params(*semantics):
    return pltpu.CompilerParams(dimension_semantics=semantics, vmem_limit_bytes=VMEM_LIMIT)


def _run_pipelined(stages):
    ahead = stages[0][0]()
    for i, (_, finish) in enumerate(stages):
        result = ahead
        if i + 1 < len(stages):
            ahead = stages[i + 1][0]()
        finish(result)


def _rope_table_kernel(pos_ref, inv_r_ref, sel_cos_ref, sel_sin_ref,
                       cos_r_ref, sin_r_ref, cos_m_ref, sin_m_ref):
    pos = pos_ref[...].astype(F32)
    ang = pos * inv_r_ref[...]
    cos = jnp.cos(ang)
    sin = jnp.sin(ang)
    cos_r_ref[...] = cos
    sin_r_ref[...] = sin
    pick = lambda x, sel_ref: jnp.dot(x, sel_ref[...], precision=lax.Precision.HIGHEST,
                                      preferred_element_type=F32)
    cos_m_ref[...] = pick(cos, sel_cos_ref)
    sin_m_ref[...] = pick(sin, sel_sin_ref)


def _rope_tables(positions):
    n = positions.size
    rows = 2048
    pos = positions.reshape(n, 1)
    half_r = RET_DK // 2
    half_m = MLA_ROPE // 2
    stride = half_r // half_m
    assert half_r == LANES and half_r % half_m == 0
    inv_r = (ROPE_BASE ** (-jnp.arange(half_r, dtype=F32) / half_r)).reshape(1, LANES)
    src = stride * jnp.arange(half_m)
    sel_cos = jnp.zeros((LANES, LANES), F32).at[src, jnp.arange(half_m)].set(1.0)
    sel_cos = sel_cos.at[src, half_m + jnp.arange(half_m)].set(1.0)
    sel_sin = jnp.zeros((LANES, LANES), F32).at[src, jnp.arange(half_m)].set(-1.0)
    sel_sin = sel_sin.at[src, half_m + jnp.arange(half_m)].set(1.0)
    const = pl.BlockSpec((1, LANES), lambda i: (0, 0))
    sel = pl.BlockSpec((LANES, LANES), lambda i: (0, 0))
    tab = pl.BlockSpec((rows, LANES), lambda i: (i, 0))
    out = jax.ShapeDtypeStruct((n, LANES), F32)
    return pl.pallas_call(
        _rope_table_kernel,
        grid=(n // rows,),
        in_specs=[pl.BlockSpec((rows, 1), lambda i: (i, 0)), const, sel, sel],
        out_specs=[tab, tab, tab, tab],
        out_shape=[out, out, out, out],
        compiler_params=_params("parallel"),
        name="rope_tables",
    )(pos, inv_r, sel_cos, sel_sin)


def _residual_stream(prev_refs, h_ref, hout_ref):
    if not prev_refs:
        return h_ref[...]
    y_ref, wout_ref = prev_refs
    h = h_ref[...] + _dot(y_ref[...], wout_ref[...])
    hout_ref[...] = h
    return h


def _split_refs(fused, refs, n_in):
    refs = list(refs)
    prev = (refs.pop(0), refs.pop(0)) if fused else ()
    ins, outs = refs[:n_in], refs[n_in:]
    hout_ref = outs.pop(0) if fused else None
    return prev, hout_ref, ins, outs


def _prev_specs(prev, row_block):
    if prev is None:
        return [], [], [], []
    y, w_out = prev
    n, width = y.shape
    return ([y, w_out.astype(BF16)], [row_block(width), _resident((width, D_MODEL))],
            [row_block(D_MODEL)], [jax.ShapeDtypeStruct((n, D_MODEL), F32)])


def _lane_scale(x, col):
    return jnp.concatenate([x[:, t * LANES:(t + 1) * LANES] * col for t in range(x.shape[1] // LANES)], axis=1)


def _ret_layer_kernel(fused, cd_ref, *refs):
    prev, hout_ref, ins, (y_ref, q_s, k_s, v_s, gmul_s, state_ref) = _split_refs(fused, refs, 9)
    h_ref, g_ref, w_ref, cos_ref, sin_ref, gn_ref, dmask_ref, qdec_ref, kdec_ref = ins

    @pl.when(pl.program_id(1) == 0)
    def _():
        state_ref[...] = jnp.zeros_like(state_ref)

    xn = (_rms(_residual_stream(prev, h_ref, hout_ref)) * g_ref[...]).astype(BF16)
    half, width, blk = RET_DK // 2, RET_DK, RET_BLOCK

    def project(col0):
        return lambda: _dot(xn, w_ref[:, col0:col0 + width])

    def store_gmul(c):
        def finish(gate):
            gmul_s[:, c:c + width] = (gn_ref[:, c:c + width] * _silu(gate)).astype(BF16)
        return finish

    def store_roped(out_ref, c, scale):
        def finish(r):
            x1, x2, cos, sin = r[:, :half], r[:, half:], cos_ref[...], sin_ref[...]
            out_ref[:, c:c + half] = ((x1 * cos - x2 * sin) * scale).astype(BF16)
            out_ref[:, c + half:c + RET_DK] = ((x2 * cos + x1 * sin) * scale).astype(BF16)
        return finish

    def store_v(c):
        def finish(v):
            v_s[:, c:c + width] = v.astype(BF16)
        return finish

    def head_projection(hd):
        v_cols = range(hd * RET_DV, (hd + 1) * RET_DV, width)
        return ([(project(2 * RET_QK_W + RET_V_W + c), store_gmul(c)) for c in v_cols]
                + [(project(2 * RET_QK_W + c), store_v(c)) for c in v_cols]
                + [(project(hd * RET_DK), store_roped(q_s, hd * RET_DK, 1.0)),
                   (project(RET_QK_W + hd * RET_DK), store_roped(k_s, hd * RET_DK, RET_DK ** -0.5))])

    def recurrence(r0, hd):
        def tiles():
            qc = q_s[r0:r0 + blk, hd * RET_DK:(hd + 1) * RET_DK]
            kc = k_s[r0:r0 + blk, hd * RET_DK:(hd + 1) * RET_DK]
            vc = v_s[r0:r0 + blk, hd * RET_DV:(hd + 1) * RET_DV]
            return qc, kc, vc

        def matmuls():
            qc, kc, vc = tiles()
            s_raw = _dot_nt(qc, kc)
            cross = _dot(qc, state_ref[hd].astype(BF16))
            kd = _lane_scale(kc.astype(F32), kdec_ref[hd]).astype(BF16)
            return s_raw, cross, _dot_tn(kd, vc)

        def finish(result):
            s_raw, cross, kv = result
            s = (s_raw * dmask_ref[hd]).astype(BF16)
            o = _dot(s, tiles()[2]) + _lane_scale(cross, qdec_ref[hd])
            state_ref[hd] = state_ref[hd] * cd_ref[hd] + kv
            gmul = gmul_s[r0:r0 + blk, hd * RET_DV:(hd + 1) * RET_DV].astype(F32)
            y_ref[r0:r0 + blk, hd * RET_DV:(hd + 1) * RET_DV] = (_rms(o) * gmul).astype(BF16)

        return matmuls, finish

    pairs = [(hd, hd + 1) for hd in range(0, RET_HEADS, 2)]
    stages = head_projection(pairs[0][0]) + head_projection(pairs[0][1])
    for p, pair in enumerate(pairs):
        items = [recurrence(b * blk, hd) for b in range(RET_ROWS // blk) for hd in pair]
        upcoming = [] if p + 1 == len(pairs) else (
            head_projection(pairs[p + 1][0]) + head_projection(pairs[p + 1][1]))
        per_item = -(-len(upcoming) // len(items))
        for i, item in enumerate(items):
            stages.append(item)
            stages += upcoming[i * per_item:(i + 1) * per_item]
    _run_pipelined(stages)


def _ret_layer(h, prev, gain, w_in, gn, cos_r, sin_r, batch, seq):
    n = h.shape[0]
    rows, blk = RET_ROWS, RET_BLOCK
    steps = seq // rows
    in_w = w_in.shape[1]
    heads = jnp.arange(RET_HEADS, dtype=F32)
    log_g = jnp.log(1.0 - 2.0 ** (-5.0 - heads))
    idx = jnp.arange(blk, dtype=F32)
    diff = idx[:, None] - idx[None, :]
    dmask = jnp.where(diff[None] >= 0, jnp.exp(jnp.maximum(diff, 0.0)[None] * log_g[:, None, None]), 0.0)
    q_decay = jnp.exp((idx + 1.0)[None, :] * log_g[:, None])
    k_decay = jnp.exp((blk - 1.0 - idx)[None, :] * log_g[:, None])
    block_decay = jnp.exp(blk * log_g)
    qdec = jnp.broadcast_to(q_decay[:, :, None], (RET_HEADS, blk, LANES))
    kdec = jnp.broadcast_to(k_decay[:, :, None], (RET_HEADS, blk, LANES))
    row_block = lambda w: pl.BlockSpec((rows, w), lambda b, i: (b * steps + i, 0))
    table = _resident((RET_HEADS, blk, LANES))
    prev_args, prev_in_specs, prev_out_specs, prev_out_shape = _prev_specs(prev, row_block)
    outs = pl.pallas_call(
        functools.partial(_ret_layer_kernel, prev is not None),
        grid=(batch, steps),
        in_specs=[pl.BlockSpec(memory_space=pltpu.SMEM)] + prev_in_specs + [
            row_block(D_MODEL), _resident((1, D_MODEL)), _resident((D_MODEL, in_w)),
            row_block(LANES), row_block(LANES), _resident((1, RET_V_W)),
            _resident((RET_HEADS, blk, blk)), table, table],
        out_specs=prev_out_specs + [row_block(RET_V_W)],
        out_shape=prev_out_shape + [jax.ShapeDtypeStruct((n, RET_V_W), BF16)],
        scratch_shapes=[pltpu.VMEM((rows, RET_QK_W), BF16), pltpu.VMEM((rows, RET_QK_W), BF16),
                        pltpu.VMEM((rows, RET_V_W), BF16), pltpu.VMEM((rows, RET_V_W), BF16),
                        pltpu.VMEM((RET_HEADS, RET_DK, RET_DV), F32)],
        compiler_params=_params("parallel", "arbitrary"),
        name="ret_layer",
    )(block_decay, *prev_args, h, gain.reshape(1, D_MODEL), w_in.astype(BF16), cos_r, sin_r,
      gn.reshape(1, RET_V_W), dmask, qdec, kdec)
    return ([h] if prev is None else []) + list(outs)


def _mla_inproj_kernel(y_ref, wout_ref, h_ref, g_ref, wlat_ref, wgate_ref, qn_ref, kvn_ref, wq_ref, wk_ref, wv_ref,
                       cos_ref, sin_ref, hout_ref, q_ref, k_ref, vt_ref, gmul_ref):
    xn = (_rms(_residual_stream((y_ref, wout_ref), h_ref, hout_ref)) * g_ref[...]).astype(BF16)
    group = 4
    groups = range(MLA_HEADS // group)
    latent = {}

    def rope(x):
        swapped = pltpu.roll(x, MLA_ROPE // 2, 1) + pltpu.roll(x, LANES - MLA_ROPE // 2, 1)
        return x * cos_ref[...] + swapped * sin_ref[...]

    def finish_latent(lat):
        q_lat = lat[:, :MLA_Q_RANK]
        kv_lat = lat[:, MLA_Q_RANK:MLA_Q_RANK + MLA_KV_RANK]
        latent["qn"] = (_rms(q_lat) * qn_ref[...]).astype(BF16)
        latent["kvn"] = (_rms(kv_lat) * kvn_ref[...]).astype(BF16)
        latent["k_rope"] = rope(lat[:, MLA_Q_RANK + MLA_KV_RANK:]).astype(BF16)

    def store_gmul(g):
        w = group * MLA_V
        def finish(gate):
            gmul_ref[:, g * w:(g + 1) * w] = _silu(gate).astype(BF16)
        return lambda: _dot(xn, wgate_ref[:, g * w:(g + 1) * w]), finish

    def store_q(g):
        w = group * MLA_HEAD_PAD
        def finish(qg):
            for j in range(group):
                c = j * MLA_HEAD_PAD
                q_ref[:, g * w + c:g * w + c + MLA_NOPE] = (qg[:, c:c + MLA_NOPE] * ATT_Q_SCALE).astype(BF16)
                q_ref[:, g * w + c + MLA_NOPE:g * w + c + MLA_HEAD_PAD] = (
                    rope(qg[:, c + MLA_NOPE:c + MLA_HEAD_PAD]) * ATT_Q_SCALE).astype(BF16)
        return lambda: _dot(latent["qn"], wq_ref[:, g * w:(g + 1) * w]), finish

    def store_k(g):
        w = group * MLA_NOPE
        def finish(kg):
            for j in range(group):
                c = (g * group + j) * MLA_HEAD_PAD
                k_ref[:, c:c + MLA_NOPE] = kg[:, j * MLA_NOPE:(j + 1) * MLA_NOPE].astype(BF16)
                k_ref[:, c + MLA_NOPE:c + MLA_HEAD_PAD] = latent["k_rope"]
        return lambda: _dot(latent["kvn"], wk_ref[:, g * w:(g + 1) * w]), finish

    def store_vt(g):
        w = group * MLA_V
        def finish(vt):
            vt_ref[g * w:(g + 1) * w, :] = vt.astype(BF16)
        return lambda: _dot_nt(wv_ref[g * w:(g + 1) * w, :], latent["kvn"]), finish

    stages = [(lambda: _dot(xn, wlat_ref[...]), finish_latent)]
    stages += [store_gmul(g) for g in groups]
    stages += [store_q(g) for g in groups] + [store_k(g) for g in groups] + [store_vt(g) for g in groups]
    _run_pipelined(stages)


def _mla_inproj(h, prev, gain, w_in, q_norm, w_q_b, kv_norm, w_kv_b, cos_m, sin_m):
    n = h.shape[0]
    rows = ROW_TILE
    row_block = lambda w: pl.BlockSpec((rows, w), lambda i: (i, 0))
    prev_args, prev_in_specs, prev_out_specs, prev_out_shape = _prev_specs(prev, row_block)
    w_lat = jnp.pad(w_in[:, :MLA_LAT_W], ((0, 0), (0, MLA_LAT_PAD - MLA_LAT_W))).astype(BF16)
    w_gate = w_in[:, MLA_LAT_W:].astype(BF16)
    w_q = jnp.pad(w_q_b.reshape(MLA_Q_RANK, MLA_HEADS, MLA_QK),
                  ((0, 0), (0, 0), (0, MLA_HEAD_PAD - MLA_QK))).reshape(MLA_Q_RANK, MLA_HEADS * MLA_HEAD_PAD)
    w_kv = w_kv_b.reshape(MLA_KV_RANK, MLA_HEADS, MLA_NOPE + MLA_V)
    w_k = w_kv[:, :, :MLA_NOPE].reshape(MLA_KV_RANK, MLA_HEADS * MLA_NOPE)
    w_vt = w_kv[:, :, MLA_NOPE:].reshape(MLA_KV_RANK, MLA_V_W).T
    qk_w = MLA_HEADS * MLA_HEAD_PAD
    return pl.pallas_call(
        _mla_inproj_kernel,
        grid=(n // rows,),
        in_specs=prev_in_specs + [
            row_block(D_MODEL), _resident((1, D_MODEL)),
            _resident((D_MODEL, MLA_LAT_PAD)), _resident((D_MODEL, MLA_V_W)),
            _resident((1, MLA_Q_RANK)), _resident((1, MLA_KV_RANK)),
            _resident((MLA_Q_RANK, qk_w)), _resident((MLA_KV_RANK, MLA_HEADS * MLA_NOPE)),
            _resident((MLA_V_W, MLA_KV_RANK)),
            row_block(LANES), row_block(LANES)],
        out_specs=prev_out_specs + [row_block(qk_w), row_block(qk_w),
                                    pl.BlockSpec((MLA_V_W, rows), lambda i: (0, i)), row_block(MLA_V_W)],
        out_shape=prev_out_shape + [jax.ShapeDtypeStruct((n, qk_w), BF16), jax.ShapeDtypeStruct((n, qk_w), BF16),
                                    jax.ShapeDtypeStruct((MLA_V_W, n), BF16),
                                    jax.ShapeDtypeStruct((n, MLA_V_W), BF16)],
        compiler_params=_params("parallel"),
        name="mla_inproj",
    )(*prev_args, h, gain.reshape(1, D_MODEL), w_lat, w_gate, q_norm.reshape(1, MLA_Q_RANK),
      kv_norm.reshape(1, MLA_KV_RANK), w_q.astype(BF16), w_k.astype(BF16), w_vt.astype(BF16), cos_m, sin_m)


def _mla_attn_kernel(q_ref, k_ref, vt_ref, gmul_ref, y_ref):
    seq = q_ref.shape[0]
    tq, tk, half = ATT_Q_TILE, ATT_K_TILE, ATT_Q_TILE // 2
    blocks = []
    for hd in range(ATT_HEADS_PER_STEP):
        for qi in range(seq // tq):
            blocks += [(hd, qi, kj * tk, tk, 0, tq, False) for kj in range(qi * tq // tk)]
            blocks += [(hd, qi, qi * tq, half, 0, tq, True), (hd, qi, qi * tq + half, half, half, half, True)]

    def scores(hd, qi, k0, kl, c0, cl, diag):
        cols = slice(hd * MLA_HEAD_PAD, (hd + 1) * MLA_HEAD_PAD)
        q = q_ref[qi * tq + c0:qi * tq + c0 + cl, cols]
        return _dot_nt(k_ref[k0:k0 + kl, cols], q)

    pending = [scores(*blk) for blk in blocks[:ATT_PREFETCH]]
    m = l = acc = None
    for n, (hd, qi, k0, kl, c0, cl, diag) in enumerate(blocks):
        s = pending.pop(0)
        if n + ATT_PREFETCH < len(blocks):
            pending.append(scores(*blocks[n + ATT_PREFETCH]))
        if diag:
            key_pos = lax.broadcasted_iota(jnp.int32, (kl, cl), 0) + (k0 - qi * tq - c0)
            s = jnp.where(key_pos <= lax.broadcasted_iota(jnp.int32, (kl, cl), 1), s, MASK_VALUE)
        vt = vt_ref[hd * MLA_V:(hd + 1) * MLA_V, k0:k0 + kl]
        m_blk = jnp.max(s, axis=0, keepdims=True)
        if k0 == 0:
            m = m_blk
            p = jnp.exp2(s - m)
            l = jnp.sum(p, axis=0, keepdims=True)
            acc = _dot(vt, p.astype(BF16))
        else:
            m_old, l_old, acc_old = m[:, c0:], l[:, c0:], acc[:, c0:]
            m_new = jnp.maximum(m_old, m_blk)
            p = jnp.exp2(s - m_new)
            alpha = jnp.exp2(m_old - m_new)
            l_new = alpha * l_old + jnp.sum(p, axis=0, keepdims=True)
            acc_new = alpha * acc_old + _dot(vt, p.astype(BF16))
            if c0:
                m_new = jnp.concatenate([m[:, :c0], m_new], axis=1)
                l_new = jnp.concatenate([l[:, :c0], l_new], axis=1)
                acc_new = jnp.concatenate([acc[:, :c0], acc_new], axis=1)
            m, l, acc = m_new, l_new, acc_new
        if k0 + kl == (qi + 1) * tq:
            o = (acc * (1.0 / l)).T
            gmul = gmul_ref[qi * tq:(qi + 1) * tq, hd * MLA_V:(hd + 1) * MLA_V].astype(F32)
            y_ref[qi * tq:(qi + 1) * tq, hd * MLA_V:(hd + 1) * MLA_V] = (o * gmul).astype(BF16)


def _mla_attn(q, k, vt, gmul, batch, seq):
    n = q.shape[0]
    hps = ATT_HEADS_PER_STEP
    return pl.pallas_call(
        _mla_attn_kernel,
        grid=(batch, MLA_HEADS // hps),
        in_specs=[pl.BlockSpec((seq, hps * MLA_HEAD_PAD), lambda b, h: (b, h)),
                  pl.BlockSpec((seq, hps * MLA_HEAD_PAD), lambda b, h: (b, h)),
                  pl.BlockSpec((hps * MLA_V, seq), lambda b, h: (h, b)),
                  pl.BlockSpec((seq, hps * MLA_V), lambda b, h: (b, h))],
        out_specs=pl.BlockSpec((seq, hps * MLA_V), lambda b, h: (b, h)),
        out_shape=jax.ShapeDtypeStruct((n, MLA_V_W), BF16),
        compiler_params=_params("parallel", "parallel"),
        name="mla_attn",
    )(q, k, vt, gmul)


def _final_proj_kernel(y_ref, w_ref, h_ref, g_ref, o_ref):
    o_ref[...] = _rms(h_ref[...] + _dot(y_ref[...], w_ref[...])) * g_ref[...]


def _final_proj(y, w_out, h, final_gain):
    n, width = y.shape
    rows = ROW_TILE
    row_block = lambda w: pl.BlockSpec((rows, w), lambda i: (i, 0))
    return pl.pallas_call(
        _final_proj_kernel,
        grid=(n // rows,),
        in_specs=[row_block(width), _resident((width, D_MODEL)), row_block(D_MODEL), _resident((1, D_MODEL))],
        out_specs=row_block(D_MODEL),
        out_shape=jax.ShapeDtypeStruct((n, D_MODEL), F32),
        compiler_params=_params("parallel"),
        name="final_proj",
    )(y, w_out.astype(BF16), h, final_gain.reshape(1, D_MODEL))


def kernel(x, positions, l0_norm, l0_ret_w_in, l0_ret_gn, l0_ret_w_out, l1_norm, l1_mla_w_in, l1_mla_q_norm, l1_mla_w_q_b, l1_mla_kv_norm, l1_mla_w_kv_b, l1_mla_w_out, l2_norm, l2_ret_w_in, l2_ret_gn, l2_ret_w_out, l3_norm, l3_mla_w_in, l3_mla_q_norm, l3_mla_w_q_b, l3_mla_kv_norm, l3_mla_w_kv_b, l3_mla_w_out, final_norm):
    batch, seq, d = x.shape
    assert d == D_MODEL and seq % ATT_Q_TILE == 0 and ATT_Q_TILE % ATT_K_TILE == 0
    assert seq % RET_ROWS == 0 and RET_ROWS % RET_BLOCK == 0 and RET_HEADS > 1 and (batch * seq) % ROW_TILE == 0
    h = x.reshape(batch * seq, d)
    cos_r, sin_r, cos_m, sin_m = _rope_tables(positions)

    def retention(h, prev, norm, w_in, gn):
        return _ret_layer(h, prev, norm, w_in, gn, cos_r, sin_r, batch, seq)

    def latent_attention(h, prev, norm, w_in, q_norm, w_q_b, kv_norm, w_kv_b):
        h, q, k, vt, gmul = _mla_inproj(h, prev, norm, w_in, q_norm, w_q_b, kv_norm, w_kv_b, cos_m, sin_m)
        return h, _mla_attn(q, k, vt, gmul, batch, seq)

    h, y = retention(h, None, l0_norm, l0_ret_w_in, l0_ret_gn)
    h, y = latent_attention(h, (y, l0_ret_w_out), l1_norm, l1_mla_w_in, l1_mla_q_norm, l1_mla_w_q_b,
                            l1_mla_kv_norm, l1_mla_w_kv_b)
    h, y = retention(h, (y, l1_mla_w_out), l2_norm, l2_ret_w_in, l2_ret_gn)
    h, y = latent_attention(h, (y, l2_ret_w_out), l3_norm, l3_mla_w_in, l3_mla_q_norm, l3_mla_w_q_b,
                            l3_mla_kv_norm, l3_mla_w_kv_b)
    h = _final_proj(y, l3_mla_w_out, h, final_norm)
    return h.reshape(batch, seq, d)
```

```python
import functools
import math

import jax
import jax.numpy as jnp
from jax import lax
from jax.experimental import pallas as pl
from jax.experimental.pallas import tpu as pltpu

D_MODEL = 1024
EPS = 1e-6
ROPE_BASE = 10000.0

RET_HEADS = 4
RET_DK = 256
RET_DV = 512
RET_QK_W = RET_HEADS * RET_DK
RET_V_W = RET_HEADS * RET_DV

MLA_HEADS = 16
MLA_Q_RANK = 256
MLA_KV_RANK = 128
MLA_NOPE = 128
MLA_ROPE = 64
MLA_V = 128
MLA_QK = MLA_NOPE + MLA_ROPE
MLA_V_W = MLA_HEADS * MLA_V
MLA_LAT_W = MLA_Q_RANK + MLA_KV_RANK + MLA_ROPE
MLA_LAT_PAD = 512
MLA_HEAD_PAD = 256

LANES = 128
VMEM_LIMIT = 56 * 1024 * 1024

ROW_TILE = 512
RET_ROWS = 512
RET_BLOCK = 256
ATT_Q_TILE = 512
ATT_K_TILE = 512
ATT_PREFETCH = 3
ATT_HEADS_PER_STEP = 2
ATT_SUM_ROWS = 16

BF16 = jnp.bfloat16
F32 = jnp.float32

ATT_Q_SCALE = (MLA_QK ** -0.5) * math.log2(math.e)
MASK_VALUE = -1e30


def _dot(a, b):
    return jnp.dot(a, b, preferred_element_type=F32)


def _dot_nt(a, b):
    return lax.dot_general(a, b, (((1,), (1,)), ((), ())), preferred_element_type=F32)


def _dot_tn(a, b):
    return lax.dot_general(a, b, (((0,), (0,)), ((), ())), preferred_element_type=F32)


def _rms(x):
    return x * lax.rsqrt(jnp.mean(x * x, axis=-1, keepdims=True) + EPS)


def _silu(g):
    half = 0.5 * g
    return half + half * jnp.tanh(half)


def _resident(shape):
    return pl.BlockSpec(shape, lambda *_: (0,) * len(shape), pipeline_mode=pl.Buffered(1))


def _params(*semantics):
    return pltpu.CompilerParams(dimension_semantics=semantics, vmem_limit_bytes=VMEM_LIMIT)


def _run_pipelined(stages):
    ahead = stages[0][0]()
    for i, (_, finish) in enumerate(stages):
        result = ahead
        if i + 1 < len(stages):
            ahead = stages[i + 1][0]()
        finish(result)


def _rope_table_kernel(pos0_ref, inv_r_ref, inv_m_ref, sign_m_ref,
                       cos_r_ref, sin_r_ref, cos_m_ref, sin_m_ref, base_ref):
    b = pl.program_id(0)
    tables = ((inv_r_ref, cos_r_ref, sin_r_ref, None), (inv_m_ref, cos_m_ref, sin_m_ref, sign_m_ref))

    @pl.when(b == 0)
    def _():
        idx = lax.broadcasted_iota(jnp.int32, cos_r_ref.shape, 0).astype(F32)
        for t, (inv_ref, _, _, _) in enumerate(tables):
            ang = idx * inv_ref[...]
            base_ref[2 * t] = jnp.cos(ang)
            base_ref[2 * t + 1] = jnp.sin(ang)

    offset = pos0_ref[b].astype(F32)
    for t, (inv_ref, cos_ref, sin_ref, sign_ref) in enumerate(tables):
        ang = offset * inv_ref[...]
        cos_o, sin_o = jnp.cos(ang), jnp.sin(ang)
        cos_s, sin_s = base_ref[2 * t], base_ref[2 * t + 1]
        cos_ref[...] = cos_s * cos_o - sin_s * sin_o
        sin = sin_s * cos_o + cos_s * sin_o
        sin_ref[...] = sin if sign_ref is None else sin * sign_ref[...]


def _rope_tables(positions):
    batch, seq = positions.shape
    half_r = RET_DK // 2
    half_m = MLA_ROPE // 2
    assert half_r == LANES
    inv_r = (ROPE_BASE ** (-jnp.arange(half_r, dtype=F32) / half_r)).reshape(1, LANES)
    inv_m_half = ROPE_BASE ** (-jnp.arange(half_m, dtype=F32) / half_m)
    zeros = jnp.zeros((LANES - 2 * half_m,), F32)
    inv_m = jnp.concatenate([inv_m_half, inv_m_half, zeros]).reshape(1, LANES)
    sign_m = jnp.concatenate([-jnp.ones((half_m,), F32), jnp.ones((half_m,), F32), zeros]).reshape(1, LANES)
    const = pl.BlockSpec((1, LANES), lambda b: (0, 0))
    tab = pl.BlockSpec((seq, LANES), lambda b: (b, 0))
    out = jax.ShapeDtypeStruct((batch * seq, LANES), F32)
    return pl.pallas_call(
        _rope_table_kernel,
        grid=(batch,),
        in_specs=[pl.BlockSpec(memory_space=pltpu.SMEM), const, const, const],
        out_specs=[tab, tab, tab, tab],
        out_shape=[out, out, out, out],
        scratch_shapes=[pltpu.VMEM((4, seq, LANES), F32)],
        compiler_params=_params("arbitrary"),
        name="rope_tables",
    )(positions[:, 0], inv_r, inv_m, sign_m)


def _residual_stream(prev_refs, h_ref, hout_ref):
    if not prev_refs:
        return h_ref[...]
    y_ref, wout_ref = prev_refs
    h = h_ref[...] + _dot(y_ref[...], wout_ref[...])
    hout_ref[...] = h
    return h


def _split_refs(fused, refs, n_in):
    refs = list(refs)
    prev = (refs.pop(0), refs.pop(0)) if fused else ()
    ins, outs = refs[:n_in], refs[n_in:]
    hout_ref = outs.pop(0) if fused else None
    return prev, hout_ref, ins, outs


def _prev_specs(prev, row_block):
    if prev is None:
        return [], [], [], []
    y, w_out = prev
    n, width = y.shape
    return ([y, w_out.astype(BF16)], [row_block(width), _resident((width, D_MODEL))],
            [row_block(D_MODEL)], [jax.ShapeDtypeStruct((n, D_MODEL), F32)])


def _lane_scale(x, col):
    return jnp.concatenate([x[:, t * LANES:(t + 1) * LANES] * col for t in range(x.shape[1] // LANES)], axis=1)


def _ret_layer_kernel(fused, cd_ref, *refs):
    prev, hout_ref, ins, (y_ref, q_s, k_s, v_s, gmul_s, state_ref) = _split_refs(fused, refs, 9)
    h_ref, g_ref, w_ref, cos_ref, sin_ref, gn_ref, dmask_ref, qdec_ref, kdec_ref = ins

    @pl.when(pl.program_id(1) == 0)
    def _():
        state_ref[...] = jnp.zeros_like(state_ref)

    xn = (_rms(_residual_stream(prev, h_ref, hout_ref)) * g_ref[...]).astype(BF16)
    half, width, blk = RET_DK // 2, RET_DK, RET_BLOCK

    def project(col0):
        return lambda: _dot(xn, w_ref[:, col0:col0 + width])

    def store_gmul(c):
        def finish(gate):
            gmul_s[:, c:c + width] = (gn_ref[:, c:c + width] * _silu(gate)).astype(BF16)
        return finish

    def store_roped(out_ref, c, scale):
        def finish(r):
            x1, x2, cos, sin = r[:, :half], r[:, half:], cos_ref[...], sin_ref[...]
            out_ref[:, c:c + half] = ((x1 * cos - x2 * sin) * scale).astype(BF16)
            out_ref[:, c + half:c + RET_DK] = ((x2 * cos + x1 * sin) * scale).astype(BF16)
        return finish

    def store_v(c):
        def finish(v):
            v_s[:, c:c + width] = v.astype(BF16)
        return finish

    def head_projection(hd):
        v_cols = range(hd * RET_DV, (hd + 1) * RET_DV, width)
        return ([(project(2 * RET_QK_W + RET_V_W + c), store_gmul(c)) for c in v_cols]
                + [(project(2 * RET_QK_W + c), store_v(c)) for c in v_cols]
                + [(project(hd * RET_DK), store_roped(q_s, hd * RET_DK, 1.0)),
                   (project(RET_QK_W + hd * RET_DK), store_roped(k_s, hd * RET_DK, RET_DK ** -0.5))])

    def recurrence(r0, hd):
        def tiles():
            qc = q_s[r0:r0 + blk, hd * RET_DK:(hd + 1) * RET_DK]
            kc = k_s[r0:r0 + blk, hd * RET_DK:(hd + 1) * RET_DK]
            vc = v_s[r0:r0 + blk, hd * RET_DV:(hd + 1) * RET_DV]
            return qc, kc, vc

        def matmuls():
            qc, kc, vc = tiles()
            s_raw = _dot_nt(qc, kc)
            cross = _dot(qc, state_ref[hd].astype(BF16))
            kd = _lane_scale(kc.astype(F32), kdec_ref[hd]).astype(BF16)
            return s_raw, cross, _dot_tn(kd, vc)

        def finish(result):
            s_raw, cross, kv = result
            s = (s_raw * dmask_ref[hd]).astype(BF16)
            o = _dot(s, tiles()[2]) + _lane_scale(cross, qdec_ref[hd])
            state_ref[hd] = state_ref[hd] * cd_ref[hd] + kv
            gmul = gmul_s[r0:r0 + blk, hd * RET_DV:(hd + 1) * RET_DV].astype(F32)
            y_ref[r0:r0 + blk, hd * RET_DV:(hd + 1) * RET_DV] = (_rms(o) * gmul).astype(BF16)

        return matmuls, finish

    pairs = [(hd, hd + 1) for hd in range(0, RET_HEADS, 2)]
    stages = head_projection(pairs[0][0]) + head_projection(pairs[0][1])
    for p, pair in enumerate(pairs):
        items = [recurrence(b * blk, hd) for b in range(RET_ROWS // blk) for hd in pair]
        upcoming = [] if p + 1 == len(pairs) else (
            head_projection(pairs[p + 1][0]) + head_projection(pairs[p + 1][1]))
        per_item = -(-len(upcoming) // len(items))
        for i, item in enumerate(items):
            stages.append(item)
            stages += upcoming[i * per_item:(i + 1) * per_item]
    _run_pipelined(stages)


def _ret_layer(h, prev, gain, w_in, gn, cos_r, sin_r, batch, seq):
    n = h.shape[0]
    rows, blk = RET_ROWS, RET_BLOCK
    steps = seq // rows
    in_w = w_in.shape[1]
    heads = jnp.arange(RET_HEADS, dtype=F32)
    log_g = jnp.log(1.0 - 2.0 ** (-5.0 - heads))
    idx = jnp.arange(blk, dtype=F32)
    diff = idx[:, None] - idx[None, :]
    dmask = jnp.where(diff[None] >= 0, jnp.exp(jnp.maximum(diff, 0.0)[None] * log_g[:, None, None]), 0.0)
    q_decay = jnp.exp((idx + 1.0)[None, :] * log_g[:, None])
    k_decay = jnp.exp((blk - 1.0 - idx)[None, :] * log_g[:, None])
    block_decay = jnp.exp(blk * log_g)
    qdec = jnp.broadcast_to(q_decay[:, :, None], (RET_HEADS, blk, LANES))
    kdec = jnp.broadcast_to(k_decay[:, :, None], (RET_HEADS, blk, LANES))
    row_block = lambda w: pl.BlockSpec((rows, w), lambda b, i: (b * steps + i, 0))
    table = _resident((RET_HEADS, blk, LANES))
    prev_args, prev_in_specs, prev_out_specs, prev_out_shape = _prev_specs(prev, row_block)
    outs = pl.pallas_call(
        functools.partial(_ret_layer_kernel, prev is not None),
        grid=(batch, steps),
        in_specs=[pl.BlockSpec(memory_space=pltpu.SMEM)] + prev_in_specs + [
            row_block(D_MODEL), _resident((1, D_MODEL)), _resident((D_MODEL, in_w)),
            row_block(LANES), row_block(LANES), _resident((1, RET_V_W)),
            _resident((RET_HEADS, blk, blk)), table, table],
        out_specs=prev_out_specs + [row_block(RET_V_W)],
        out_shape=prev_out_shape + [jax.ShapeDtypeStruct((n, RET_V_W), BF16)],
        scratch_shapes=[pltpu.VMEM((rows, RET_QK_W), BF16), pltpu.VMEM((rows, RET_QK_W), BF16),
                        pltpu.VMEM((rows, RET_V_W), BF16), pltpu.VMEM((rows, RET_V_W), BF16),
                        pltpu.VMEM((RET_HEADS, RET_DK, RET_DV), F32)],
        compiler_params=_params("parallel", "arbitrary"),
        name="ret_layer",
    )(block_decay, *prev_args, h, gain.reshape(1, D_MODEL), w_in.astype(BF16), cos_r, sin_r,
      gn.reshape(1, RET_V_W), dmask, qdec, kdec)
    return ([h] if prev is None else []) + list(outs)


def _mla_inproj_kernel(y_ref, wout_ref, h_ref, g_ref, wlat_ref, wgate_ref, qn_ref, kvn_ref, wq_ref, wk_ref, wv_ref,
                       cos_ref, sin_ref, hout_ref, q_ref, k_ref, vt_ref, gmul_ref):
    xn = (_rms(_residual_stream((y_ref, wout_ref), h_ref, hout_ref)) * g_ref[...]).astype(BF16)
    group = 4
    groups = range(MLA_HEADS // group)
    latent = {}

    def rope(x):
        swapped = pltpu.roll(x, MLA_ROPE // 2, 1) + pltpu.roll(x, LANES - MLA_ROPE // 2, 1)
        return x * cos_ref[...] + swapped * sin_ref[...]

    def finish_latent(lat):
        q_lat = lat[:, :MLA_Q_RANK]
        kv_lat = lat[:, MLA_Q_RANK:MLA_Q_RANK + MLA_KV_RANK]
        latent["qn"] = (_rms(q_lat) * qn_ref[...]).astype(BF16)
        latent["kvn"] = (_rms(kv_lat) * kvn_ref[...]).astype(BF16)
        latent["k_rope"] = rope(lat[:, MLA_Q_RANK + MLA_KV_RANK:]).astype(BF16)

    def store_gmul(g):
        w = group * MLA_V
        def finish(gate):
            gmul_ref[:, g * w:(g + 1) * w] = _silu(gate).astype(BF16)
        return lambda: _dot(xn, wgate_ref[:, g * w:(g + 1) * w]), finish

    def store_q(g):
        w = group * MLA_HEAD_PAD
        def finish(qg):
            for j in range(group):
                c = j * MLA_HEAD_PAD
                q_ref[:, g * w + c:g * w + c + MLA_NOPE] = (qg[:, c:c + MLA_NOPE] * ATT_Q_SCALE).astype(BF16)
                q_ref[:, g * w + c + MLA_NOPE:g * w + c + MLA_HEAD_PAD] = (
                    rope(qg[:, c + MLA_NOPE:c + MLA_HEAD_PAD]) * ATT_Q_SCALE).astype(BF16)
        return lambda: _dot(latent["qn"], wq_ref[:, g * w:(g + 1) * w]), finish

    def store_k(g):
        w = group * MLA_NOPE
        def finish(kg):
            for j in range(group):
                c = (g * group + j) * MLA_HEAD_PAD
                k_ref[:, c:c + MLA_NOPE] = kg[:, j * MLA_NOPE:(j + 1) * MLA_NOPE].astype(BF16)
                k_ref[:, c + MLA_NOPE:c + MLA_HEAD_PAD] = latent["k_rope"]
        return lambda: _dot(latent["kvn"], wk_ref[:, g * w:(g + 1) * w]), finish

    def store_vt(g):
        w = group * MLA_V
        def finish(vt):
            vt_ref[g * w:(g + 1) * w, :] = vt.astype(BF16)
        return lambda: _dot_nt(wv_ref[g * w:(g + 1) * w, :], latent["kvn"]), finish

    stages = [(lambda: _dot(xn, wlat_ref[...]), finish_latent)]
    stages += [store_gmul(g) for g in groups]
    stages += [store_q(g) for g in groups] + [store_k(g) for g in groups] + [store_vt(g) for g in groups]
    _run_pipelined(stages)


def _mla_inproj(h, prev, gain, w_in, q_norm, w_q_b, kv_norm, w_kv_b, cos_m, sin_m):
    n = h.shape[0]
    rows = ROW_TILE
    row_block = lambda w: pl.BlockSpec((rows, w), lambda i: (i, 0))
    prev_args, prev_in_specs, prev_out_specs, prev_out_shape = _prev_specs(prev, row_block)
    w_lat = jnp.pad(w_in[:, :MLA_LAT_W], ((0, 0), (0, MLA_LAT_PAD - MLA_LAT_W))).astype(BF16)
    w_gate = w_in[:, MLA_LAT_W:].astype(BF16)
    w_q = jnp.pad(w_q_b.reshape(MLA_Q_RANK, MLA_HEADS, MLA_QK),
                  ((0, 0), (0, 0), (0, MLA_HEAD_PAD - MLA_QK))).reshape(MLA_Q_RANK, MLA_HEADS * MLA_HEAD_PAD)
    w_kv = w_kv_b.reshape(MLA_KV_RANK, MLA_HEADS, MLA_NOPE + MLA_V)
    w_k = w_kv[:, :, :MLA_NOPE].reshape(MLA_KV_RANK, MLA_HEADS * MLA_NOPE)
    w_vt = w_kv[:, :, MLA_NOPE:].reshape(MLA_KV_RANK, MLA_V_W).T
    qk_w = MLA_HEADS * MLA_HEAD_PAD
    return pl.pallas_call(
        _mla_inproj_kernel,
        grid=(n // rows,),
        in_specs=prev_in_specs + [
            row_block(D_MODEL), _resident((1, D_MODEL)),
            _resident((D_MODEL, MLA_LAT_PAD)), _resident((D_MODEL, MLA_V_W)),
            _resident((1, MLA_Q_RANK)), _resident((1, MLA_KV_RANK)),
            _resident((MLA_Q_RANK, qk_w)), _resident((MLA_KV_RANK, MLA_HEADS * MLA_NOPE)),
            _resident((MLA_V_W, MLA_KV_RANK)),
            row_block(LANES), row_block(LANES)],
        out_specs=prev_out_specs + [row_block(qk_w), row_block(qk_w),
                                    pl.BlockSpec((MLA_V_W, rows), lambda i: (0, i)), row_block(MLA_V_W)],
        out_shape=prev_out_shape + [jax.ShapeDtypeStruct((n, qk_w), BF16), jax.ShapeDtypeStruct((n, qk_w), BF16),
                                    jax.ShapeDtypeStruct((MLA_V_W, n), BF16),
                                    jax.ShapeDtypeStruct((n, MLA_V_W), BF16)],
        compiler_params=_params("parallel"),
        name="mla_inproj",
    )(*prev_args, h, gain.reshape(1, D_MODEL), w_lat, w_gate, q_norm.reshape(1, MLA_Q_RANK),
      kv_norm.reshape(1, MLA_KV_RANK), w_q.astype(BF16), w_k.astype(BF16), w_vt.astype(BF16), cos_m, sin_m)


def _mla_attn_kernel(q_ref, k_ref, vt_ref, gmul_ref, y_ref):
    seq = q_ref.shape[0]
    tq, tk, half = ATT_Q_TILE, ATT_K_TILE, ATT_Q_TILE // 2
    blocks = []
    for hd in range(ATT_HEADS_PER_STEP):
        for qi in range(seq // tq):
            blocks += [(hd, qi, kj * tk, tk, 0, tq, False) for kj in range(qi * tq // tk)]
            blocks += [(hd, qi, qi * tq, half, 0, tq, True), (hd, qi, qi * tq + half, half, half, half, True)]

    def scores(hd, qi, k0, kl, c0, cl, diag):
        cols = slice(hd * MLA_HEAD_PAD, (hd + 1) * MLA_HEAD_PAD)
        q = q_ref[qi * tq + c0:qi * tq + c0 + cl, cols]
        return _dot_nt(k_ref[k0:k0 + kl, cols], q)

    pending = [scores(*blk) for blk in blocks[:ATT_PREFETCH]]
    m = acc = None
    for n, (hd, qi, k0, kl, c0, cl, diag) in enumerate(blocks):
        s = pending.pop(0)
        if n + ATT_PREFETCH < len(blocks):
            pending.append(scores(*blocks[n + ATT_PREFETCH]))
        if diag:
            key_pos = lax.broadcasted_iota(jnp.int32, (kl, cl), 0) + (k0 - qi * tq - c0)
            s = jnp.where(key_pos <= lax.broadcasted_iota(jnp.int32, (kl, cl), 1), s, MASK_VALUE)
        vt = jnp.concatenate([vt_ref[hd * MLA_V:(hd + 1) * MLA_V, k0:k0 + kl],
                              jnp.ones((ATT_SUM_ROWS, kl), BF16)], axis=0)
        m_blk = jnp.max(s, axis=0, keepdims=True)
        if k0 == 0:
            m = m_blk
            acc = _dot(vt, jnp.exp2((s - m).astype(BF16)))
        else:
            m_old, acc_old = m[:, c0:], acc[:, c0:]
            m_new = jnp.maximum(m_old, m_blk)
            acc_new = jnp.exp2(m_old - m_new) * acc_old + _dot(vt, jnp.exp2((s - m_new).astype(BF16)))
            if c0:
                m_new = jnp.concatenate([m[:, :c0], m_new], axis=1)
                acc_new = jnp.concatenate([acc[:, :c0], acc_new], axis=1)
            m, acc = m_new, acc_new
        if k0 + kl == (qi + 1) * tq:
            o = (acc[:MLA_V] * (1.0 / acc[MLA_V:MLA_V + 1])).T
            gmul = gmul_ref[qi * tq:(qi + 1) * tq, hd * MLA_V:(hd + 1) * MLA_V].astype(F32)
            y_ref[qi * tq:(qi + 1) * tq, hd * MLA_V:(hd + 1) * MLA_V] = (o * gmul).astype(BF16)


def _mla_attn(q, k, vt, gmul, batch, seq):
    n = q.shape[0]
    hps = ATT_HEADS_PER_STEP
    return pl.pallas_call(
        _mla_attn_kernel,
        grid=(batch, MLA_HEADS // hps),
        in_specs=[pl.BlockSpec((seq, hps * MLA_HEAD_PAD), lambda b, h: (b, h)),
                  pl.BlockSpec((seq, hps * MLA_HEAD_PAD), lambda b, h: (b, h)),
                  pl.BlockSpec((hps * MLA_V, seq), lambda b, h: (h, b)),
                  pl.BlockSpec((seq, hps * MLA_V), lambda b, h: (b, h))],
        out_specs=pl.BlockSpec((seq, hps * MLA_V), lambda b, h: (b, h)),
        out_shape=jax.ShapeDtypeStruct((n, MLA_V_W), BF16),
        compiler_params=_params("parallel", "parallel"),
        name="mla_attn",
    )(q, k, vt, gmul)


def _final_proj_kernel(y_ref, w_ref, h_ref, g_ref, o_ref):
    o_ref[...] = _rms(h_ref[...] + _dot(y_ref[...], w_ref[...])) * g_ref[...]


def _final_proj(y, w_out, h, final_gain):
    n, width = y.shape
    rows = 2 * ROW_TILE
    row_block = lambda w: pl.BlockSpec((rows, w), lambda i: (i, 0))
    return pl.pallas_call(
        _final_proj_kernel,
        grid=(n // rows,),
        in_specs=[row_block(width), _resident((width, D_MODEL)), row_block(D_MODEL), _resident((1, D_MODEL))],
        out_specs=row_block(D_MODEL),
        out_shape=jax.ShapeDtypeStruct((n, D_MODEL), F32),
        compiler_params=_params("parallel"),
        name="final_proj",
    )(y, w_out.astype(BF16), h, final_gain.reshape(1, D_MODEL))


def kernel(x, positions, l0_norm, l0_ret_w_in, l0_ret_gn, l0_ret_w_out, l1_norm, l1_mla_w_in, l1_mla_q_norm, l1_mla_w_q_b, l1_mla_kv_norm, l1_mla_w_kv_b, l1_mla_w_out, l2_norm, l2_ret_w_in, l2_ret_gn, l2_ret_w_out, l3_norm, l3_mla_w_in, l3_mla_q_norm, l3_mla_w_q_b, l3_mla_kv_norm, l3_mla_w_kv_b, l3_mla_w_out, final_norm):
    batch, seq, d = x.shape
    assert d == D_MODEL and seq % ATT_Q_TILE == 0 and ATT_Q_TILE % ATT_K_TILE == 0
    assert seq % RET_ROWS == 0 and RET_ROWS % RET_BLOCK == 0 and RET_HEADS > 1 and (batch * seq) % ROW_TILE == 0
    h = x.reshape(batch * seq, d)
    cos_r, sin_r, cos_m, sin_m = _rope_tables(positions)

    def retention(h, prev, norm, w_in, gn):
        return _ret_layer(h, prev, norm, w_in, gn, cos_r, sin_r, batch, seq)

    def latent_attention(h, prev, norm, w_in, q_norm, w_q_b, kv_norm, w_kv_b):
        h, q, k, vt, gmul = _mla_inproj(h, prev, norm, w_in, q_norm, w_q_b, kv_norm, w_kv_b, cos_m, sin_m)
        return h, _mla_attn(q, k, vt, gmul, batch, seq)

    h, y = retention(h, None, l0_norm, l0_ret_w_in, l0_ret_gn)
    h, y = latent_attention(h, (y, l0_ret_w_out), l1_norm, l1_mla_w_in, l1_mla_q_norm, l1_mla_w_q_b,
                            l1_mla_kv_norm, l1_mla_w_kv_b)
    h, y = retention(h, (y, l1_mla_w_out), l2_norm, l2_ret_w_in, l2_ret_gn)
    h, y = latent_attention(h, (y, l2_ret_w_out), l3_norm, l3_mla_w_in, l3_mla_q_norm, l3_mla_w_q_b,
                            l3_mla_kv_norm, l3_mla_w_kv_b)
    h = _final_proj(y, l3_mla_w_out, h, final_norm)
    return h.reshape(batch, seq, d)
```

```python
import functools
import math

import jax
import jax.numpy as jnp
from jax import lax
from jax.experimental import pallas as pl
from jax.experimental.pallas import tpu as pltpu

D_MODEL = 1024
EPS = 1e-6
ROPE_BASE = 10000.0

RET_HEADS = 4
RET_DK = 256
RET_DV = 512
RET_QK_W = RET_HEADS * RET_DK
RET_V_W = RET_HEADS * RET_DV

MLA_HEADS = 16
MLA_Q_RANK = 256
MLA_KV_RANK = 128
MLA_NOPE = 128
MLA_ROPE = 64
MLA_V = 128
MLA_QK = MLA_NOPE + MLA_ROPE
MLA_V_W = MLA_HEADS * MLA_V
MLA_LAT_W = MLA_Q_RANK + MLA_KV_RANK + MLA_ROPE
MLA_LAT_PAD = 512
MLA_HEAD_PAD = 256

LANES = 128
VMEM_LIMIT = 56 * 1024 * 1024

ROW_TILE = 512
RET_ROWS = 512
RET_BLOCK = 256
ATT_Q_TILE = 512
ATT_K_TILE = 256
ATT_PREFETCH = 5
ATT_HEADS_PER_STEP = 2
ATT_SUM_ROWS = 16

BF16 = jnp.bfloat16
F32 = jnp.float32

ATT_Q_SCALE = (MLA_QK ** -0.5) * math.log2(math.e)
MASK_VALUE = -1e30


def _dot(a, b):
    return jnp.dot(a, b, preferred_element_type=F32)


def _dot_nt(a, b):
    return lax.dot_general(a, b, (((1,), (1,)), ((), ())), preferred_element_type=F32)


def _dot_tn(a, b):
    return lax.dot_general(a, b, (((0,), (0,)), ((), ())), preferred_element_type=F32)


def _rms(x):
    return x * lax.rsqrt(jnp.mean(x * x, axis=-1, keepdims=True) + EPS)


def _silu(g):
    half = 0.5 * g
    return half + half * jnp.tanh(half)


def _resident(shape):
    return pl.BlockSpec(shape, lambda *_: (0,) * len(shape), pipeline_mode=pl.Buffered(1))


def _params(*semantics):
    return pltpu.CompilerParams(dimension_semantics=semantics, vmem_limit_bytes=VMEM_LIMIT)


def _run_pipelined(stages):
    ahead = stages[0][0]()
    for i, (_, finish) in enumerate(stages):
        result = ahead
        if i + 1 < len(stages):
            ahead = stages[i + 1][0]()
        finish(result)


def _rope_table_kernel(pos0_ref, inv_r_ref, inv_m_ref, sign_m_ref,
                       cos_r_ref, sin_r_ref, cos_m_ref, sin_m_ref, base_ref):
    b = pl.program_id(0)
    tables = ((inv_r_ref, cos_r_ref, sin_r_ref, None), (inv_m_ref, cos_m_ref, sin_m_ref, sign_m_ref))

    @pl.when(b == 0)
    def _():
        idx = lax.broadcasted_iota(jnp.int32, cos_r_ref.shape, 0).astype(F32)
        for t, (inv_ref, _, _, _) in enumerate(tables):
            ang = idx * inv_ref[...]
            base_ref[2 * t] = jnp.cos(ang)
            base_ref[2 * t + 1] = jnp.sin(ang)

    offset = pos0_ref[b].astype(F32)
    for t, (inv_ref, cos_ref, sin_ref, sign_ref) in enumerate(tables):
        ang = offset * inv_ref[...]
        cos_o, sin_o = jnp.cos(ang), jnp.sin(ang)
        cos_s, sin_s = base_ref[2 * t], base_ref[2 * t + 1]
        cos_ref[...] = cos_s * cos_o - sin_s * sin_o
        sin = sin_s * cos_o + cos_s * sin_o
        sin_ref[...] = sin if sign_ref is None else sin * sign_ref[...]


def _rope_tables(positions):
    batch, seq = positions.shape
    half_r = RET_DK // 2
    half_m = MLA_ROPE // 2
    assert half_r == LANES
    inv_r = (ROPE_BASE ** (-jnp.arange(half_r, dtype=F32) / half_r)).reshape(1, LANES)
    inv_m_half = ROPE_BASE ** (-jnp.arange(half_m, dtype=F32) / half_m)
    zeros = jnp.zeros((LANES - 2 * half_m,), F32)
    inv_m = jnp.concatenate([inv_m_half, inv_m_half, zeros]).reshape(1, LANES)
    sign_m = jnp.concatenate([-jnp.ones((half_m,), F32), jnp.ones((half_m,), F32), zeros]).reshape(1, LANES)
    const = pl.BlockSpec((1, LANES), lambda b: (0, 0))
    tab = pl.BlockSpec((seq, LANES), lambda b: (b, 0))
    out = jax.ShapeDtypeStruct((batch * seq, LANES), F32)
    return pl.pallas_call(
        _rope_table_kernel,
        grid=(batch,),
        in_specs=[pl.BlockSpec(memory_space=pltpu.SMEM), const, const, const],
        out_specs=[tab, tab, tab, tab],
        out_shape=[out, out, out, out],
        scratch_shapes=[pltpu.VMEM((4, seq, LANES), F32)],
        compiler_params=_params("arbitrary"),
        name="rope_tables",
    )(positions[:, 0], inv_r, inv_m, sign_m)


def _residual_stream(prev_refs, h_ref, hout_ref):
    if not prev_refs:
        return h_ref[...]
    y_ref, wout_ref = prev_refs
    h = h_ref[...] + _dot(y_ref[...], wout_ref[...])
    hout_ref[...] = h
    return h


def _split_refs(fused, refs, n_in):
    refs = list(refs)
    prev = (refs.pop(0), refs.pop(0)) if fused else ()
    ins, outs = refs[:n_in], refs[n_in:]
    hout_ref = outs.pop(0) if fused else None
    return prev, hout_ref, ins, outs


def _prev_specs(prev, row_block):
    if prev is None:
        return [], [], [], []
    y, w_out = prev
    n, width = y.shape
    return ([y, w_out.astype(BF16)], [row_block(width), _resident((width, D_MODEL))],
            [row_block(D_MODEL)], [jax.ShapeDtypeStruct((n, D_MODEL), F32)])


def _lane_scale(x, col):
    return jnp.concatenate([x[:, t * LANES:(t + 1) * LANES] * col for t in range(x.shape[1] // LANES)], axis=1)


def _ret_layer_kernel(fused, cd_ref, *refs):
    prev, hout_ref, ins, (y_ref, q_s, k_s, v_s, gmul_s, state_ref) = _split_refs(fused, refs, 9)
    h_ref, g_ref, w_ref, cos_ref, sin_ref, gn_ref, dmask_ref, qdec_ref, kdec_ref = ins

    @pl.when(pl.program_id(1) == 0)
    def _():
        state_ref[...] = jnp.zeros_like(state_ref)

    xn = (_rms(_residual_stream(prev, h_ref, hout_ref)) * g_ref[...]).astype(BF16)
    half, width, blk = RET_DK // 2, RET_DK, RET_BLOCK

    def project(col0):
        return lambda: _dot(xn, w_ref[:, col0:col0 + width])

    def store_gmul(c):
        def finish(gate):
            gmul_s[:, c:c + width] = (gn_ref[:, c:c + width] * _silu(gate)).astype(BF16)
        return finish

    def store_roped(out_ref, c, scale):
        def finish(r):
            x1, x2, cos, sin = r[:, :half], r[:, half:], cos_ref[...], sin_ref[...]
            out_ref[:, c:c + half] = ((x1 * cos - x2 * sin) * scale).astype(BF16)
            out_ref[:, c + half:c + RET_DK] = ((x2 * cos + x1 * sin) * scale).astype(BF16)
        return finish

    def store_v(c):
        def finish(v):
            v_s[:, c:c + width] = v.astype(BF16)
        return finish

    def head_projection(hd):
        v_cols = range(hd * RET_DV, (hd + 1) * RET_DV, width)
        return ([(project(2 * RET_QK_W + RET_V_W + c), store_gmul(c)) for c in v_cols]
                + [(project(2 * RET_QK_W + c), store_v(c)) for c in v_cols]
                + [(project(hd * RET_DK), store_roped(q_s, hd * RET_DK, 1.0)),
                   (project(RET_QK_W + hd * RET_DK), store_roped(k_s, hd * RET_DK, RET_DK ** -0.5))])

    def recurrence(r0, hd):
        def tiles():
            qc = q_s[r0:r0 + blk, hd * RET_DK:(hd + 1) * RET_DK]
            kc = k_s[r0:r0 + blk, hd * RET_DK:(hd + 1) * RET_DK]
            vc = v_s[r0:r0 + blk, hd * RET_DV:(hd + 1) * RET_DV]
            return qc, kc, vc

        def matmuls():
            qc, kc, vc = tiles()
            s_raw = _dot_nt(qc, kc)
            cross = _dot(qc, state_ref[hd].astype(BF16))
            kd = _lane_scale(kc.astype(F32), kdec_ref[hd]).astype(BF16)
            return s_raw, cross, _dot_tn(kd, vc)

        def finish(result):
            s_raw, cross, kv = result
            s = (s_raw * dmask_ref[hd]).astype(BF16)
            o = _dot(s, tiles()[2]) + _lane_scale(cross, qdec_ref[hd])
            state_ref[hd] = state_ref[hd] * cd_ref[hd] + kv
            gmul = gmul_s[r0:r0 + blk, hd * RET_DV:(hd + 1) * RET_DV].astype(F32)
            y_ref[r0:r0 + blk, hd * RET_DV:(hd + 1) * RET_DV] = (_rms(o) * gmul).astype(BF16)

        return matmuls, finish

    pairs = [(hd, hd + 1) for hd in range(0, RET_HEADS, 2)]
    stages = head_projection(pairs[0][0]) + head_projection(pairs[0][1])
    for p, pair in enumerate(pairs):
        items = [recurrence(b * blk, hd) for b in range(RET_ROWS // blk) for hd in pair]
        upcoming = [] if p + 1 == len(pairs) else (
            head_projection(pairs[p + 1][0]) + head_projection(pairs[p + 1][1]))
        per_item = -(-len(upcoming) // len(items))
        for i, item in enumerate(items):
            stages.append(item)
            stages += upcoming[i * per_item:(i + 1) * per_item]
    _run_pipelined(stages)


def _ret_layer(h, prev, gain, w_in, gn, cos_r, sin_r, batch, seq):
    n = h.shape[0]
    rows, blk = RET_ROWS, RET_BLOCK
    steps = seq // rows
    in_w = w_in.shape[1]
    heads = jnp.arange(RET_HEADS, dtype=F32)
    log_g = jnp.log(1.0 - 2.0 ** (-5.0 - heads))
    idx = jnp.arange(blk, dtype=F32)
    diff = idx[:, None] - idx[None, :]
    dmask = jnp.where(diff[None] >= 0, jnp.exp(jnp.maximum(diff, 0.0)[None] * log_g[:, None, None]), 0.0)
    q_decay = jnp.exp((idx + 1.0)[None, :] * log_g[:, None])
    k_decay = jnp.exp((blk - 1.0 - idx)[None, :] * log_g[:, None])
    block_decay = jnp.exp(blk * log_g)
    qdec = jnp.broadcast_to(q_decay[:, :, None], (RET_HEADS, blk, LANES))
    kdec = jnp.broadcast_to(k_decay[:, :, None], (RET_HEADS, blk, LANES))
    row_block = lambda w: pl.BlockSpec((rows, w), lambda b, i: (b * steps + i, 0))
    table = _resident((RET_HEADS, blk, LANES))
    prev_args, prev_in_specs, prev_out_specs, prev_out_shape = _prev_specs(prev, row_block)
    outs = pl.pallas_call(
        functools.partial(_ret_layer_kernel, prev is not None),
        grid=(batch, steps),
        in_specs=[pl.BlockSpec(memory_space=pltpu.SMEM)] + prev_in_specs + [
            row_block(D_MODEL), _resident((1, D_MODEL)), _resident((D_MODEL, in_w)),
            row_block(LANES), row_block(LANES), _resident((1, RET_V_W)),
            _resident((RET_HEADS, blk, blk)), table, table],
        out_specs=prev_out_specs + [row_block(RET_V_W)],
        out_shape=prev_out_shape + [jax.ShapeDtypeStruct((n, RET_V_W), BF16)],
        scratch_shapes=[pltpu.VMEM((rows, RET_QK_W), BF16), pltpu.VMEM((rows, RET_QK_W), BF16),
                        pltpu.VMEM((rows, RET_V_W), BF16), pltpu.VMEM((rows, RET_V_W), BF16),
                        pltpu.VMEM((RET_HEADS, RET_DK, RET_DV), F32)],
        compiler_params=_params("parallel", "arbitrary"),
        name="ret_layer",
    )(block_decay, *prev_args, h, gain.reshape(1, D_MODEL), w_in.astype(BF16), cos_r, sin_r,
      gn.reshape(1, RET_V_W), dmask, qdec, kdec)
    return ([h] if prev is None else []) + list(outs)


def _mla_inproj_kernel(y_ref, wout_ref, h_ref, g_ref, wlat_ref, wgate_ref, qn_ref, kvn_ref, wq_ref, wk_ref, wv_ref,
                       cos_ref, sin_ref, hout_ref, q_ref, k_ref, vt_ref, gmul_ref):
    xn = (_rms(_residual_stream((y_ref, wout_ref), h_ref, hout_ref)) * g_ref[...]).astype(BF16)
    group = 4
    groups = range(MLA_HEADS // group)
    latent = {}

    def rope(x):
        swapped = pltpu.roll(x, MLA_ROPE // 2, 1) + pltpu.roll(x, LANES - MLA_ROPE // 2, 1)
        return x * cos_ref[...] + swapped * sin_ref[...]

    def finish_latent(lat):
        q_lat = lat[:, :MLA_Q_RANK]
        kv_lat = lat[:, MLA_Q_RANK:MLA_Q_RANK + MLA_KV_RANK]
        latent["qn"] = (_rms(q_lat) * qn_ref[...]).astype(BF16)
        latent["kvn"] = (_rms(kv_lat) * kvn_ref[...]).astype(BF16)
        latent["k_rope"] = rope(lat[:, MLA_Q_RANK + MLA_KV_RANK:]).astype(BF16)

    def store_gmul(g):
        w = group * MLA_V
        def finish(gate):
            gmul_ref[:, g * w:(g + 1) * w] = _silu(gate).astype(BF16)
        return lambda: _dot(xn, wgate_ref[:, g * w:(g + 1) * w]), finish

    def store_q(g):
        w = group * MLA_HEAD_PAD
        def finish(qg):
            for j in range(group):
                c = j * MLA_HEAD_PAD
                q_ref[:, g * w + c:g * w + c + MLA_NOPE] = (qg[:, c:c + MLA_NOPE] * ATT_Q_SCALE).astype(BF16)
                q_ref[:, g * w + c + MLA_NOPE:g * w + c + MLA_HEAD_PAD] = (
                    rope(qg[:, c + MLA_NOPE:c + MLA_HEAD_PAD]) * ATT_Q_SCALE).astype(BF16)
        return lambda: _dot(latent["qn"], wq_ref[:, g * w:(g + 1) * w]), finish

    def store_k(g):
        w = group * MLA_NOPE
        def finish(kg):
            for j in range(group):
                c = (g * group + j) * MLA_HEAD_PAD
                k_ref[:, c:c + MLA_NOPE] = kg[:, j * MLA_NOPE:(j + 1) * MLA_NOPE].astype(BF16)
                k_ref[:, c + MLA_NOPE:c + MLA_HEAD_PAD] = latent["k_rope"]
        return lambda: _dot(latent["kvn"], wk_ref[:, g * w:(g + 1) * w]), finish

    def store_vt(g):
        w = group * MLA_V
        def finish(vt):
            vt_ref[g * w:(g + 1) * w, :] = vt.astype(BF16)
        return lambda: _dot_nt(wv_ref[g * w:(g + 1) * w, :], latent["kvn"]), finish

    stages = [(lambda: _dot(xn, wlat_ref[...]), finish_latent)]
    stages += [store_gmul(g) for g in groups]
    stages += [store_q(g) for g in groups] + [store_k(g) for g in groups] + [store_vt(g) for g in groups]
    _run_pipelined(stages)


def _mla_inproj(h, prev, gain, w_in, q_norm, w_q_b, kv_norm, w_kv_b, cos_m, sin_m):
    n = h.shape[0]
    rows = ROW_TILE
    row_block = lambda w: pl.BlockSpec((rows, w), lambda i: (i, 0))
    prev_args, prev_in_specs, prev_out_specs, prev_out_shape = _prev_specs(prev, row_block)
    w_lat = jnp.pad(w_in[:, :MLA_LAT_W], ((0, 0), (0, MLA_LAT_PAD - MLA_LAT_W))).astype(BF16)
    w_gate = w_in[:, MLA_LAT_W:].astype(BF16)
    w_q = jnp.pad(w_q_b.reshape(MLA_Q_RANK, MLA_HEADS, MLA_QK),
                  ((0, 0), (0, 0), (0, MLA_HEAD_PAD - MLA_QK))).reshape(MLA_Q_RANK, MLA_HEADS * MLA_HEAD_PAD)
    w_kv = w_kv_b.reshape(MLA_KV_RANK, MLA_HEADS, MLA_NOPE + MLA_V)
    w_k = w_kv[:, :, :MLA_NOPE].reshape(MLA_KV_RANK, MLA_HEADS * MLA_NOPE)
    w_vt = w_kv[:, :, MLA_NOPE:].reshape(MLA_KV_RANK, MLA_V_W).T
    qk_w = MLA_HEADS * MLA_HEAD_PAD
    return pl.pallas_call(
        _mla_inproj_kernel,
        grid=(n // rows,),
        in_specs=prev_in_specs + [
            row_block(D_MODEL), _resident((1, D_MODEL)),
            _resident((D_MODEL, MLA_LAT_PAD)), _resident((D_MODEL, MLA_V_W)),
            _resident((1, MLA_Q_RANK)), _resident((1, MLA_KV_RANK)),
            _resident((MLA_Q_RANK, qk_w)), _resident((MLA_KV_RANK, MLA_HEADS * MLA_NOPE)),
            _resident((MLA_V_W, MLA_KV_RANK)),
            row_block(LANES), row_block(LANES)],
        out_specs=prev_out_specs + [row_block(qk_w), row_block(qk_w),
                                    pl.BlockSpec((MLA_V_W, rows), lambda i: (0, i)), row_block(MLA_V_W)],
        out_shape=prev_out_shape + [jax.ShapeDtypeStruct((n, qk_w), BF16), jax.ShapeDtypeStruct((n, qk_w), BF16),
                                    jax.ShapeDtypeStruct((MLA_V_W, n), BF16),
                                    jax.ShapeDtypeStruct((n, MLA_V_W), BF16)],
        compiler_params=_params("parallel"),
        name="mla_inproj",
    )(*prev_args, h, gain.reshape(1, D_MODEL), w_lat, w_gate, q_norm.reshape(1, MLA_Q_RANK),
      kv_norm.reshape(1, MLA_KV_RANK), w_q.astype(BF16), w_k.astype(BF16), w_vt.astype(BF16), cos_m, sin_m)


def _mla_attn_kernel(q_ref, k_ref, vt_ref, gmul_ref, y_ref):
    seq = q_ref.shape[0]
    tq, tk, half = ATT_Q_TILE, ATT_K_TILE, ATT_Q_TILE // 2
    blocks = []
    for hd in range(ATT_HEADS_PER_STEP):
        for qi in range(seq // tq):
            blocks += [(hd, qi, kj * tk, tk, 0, tq, False) for kj in range(qi * tq // tk)]
            blocks += [(hd, qi, qi * tq, half, 0, tq, True), (hd, qi, qi * tq + half, half, half, half, True)]

    def scores(hd, qi, k0, kl, c0, cl, diag):
        cols = slice(hd * MLA_HEAD_PAD, (hd + 1) * MLA_HEAD_PAD)
        q = q_ref[qi * tq + c0:qi * tq + c0 + cl, cols]
        return _dot_nt(k_ref[k0:k0 + kl, cols], q)

    pending = [scores(*blk) for blk in blocks[:ATT_PREFETCH]]
    m = acc = None
    for n, (hd, qi, k0, kl, c0, cl, diag) in enumerate(blocks):
        s = pending.pop(0)
        if n + ATT_PREFETCH < len(blocks):
            pending.append(scores(*blocks[n + ATT_PREFETCH]))
        if diag:
            key_pos = lax.broadcasted_iota(jnp.int32, (kl, cl), 0) + (k0 - qi * tq - c0)
            s = jnp.where(key_pos <= lax.broadcasted_iota(jnp.int32, (kl, cl), 1), s, MASK_VALUE)
        vt = jnp.concatenate([vt_ref[hd * MLA_V:(hd + 1) * MLA_V, k0:k0 + kl],
                              jnp.ones((ATT_SUM_ROWS, kl), BF16)], axis=0)
        m_blk = jnp.max(s, axis=0, keepdims=True)
        if k0 == 0:
            m = m_blk
            acc = _dot(vt, jnp.exp2((s - m).astype(BF16)))
        else:
            m_old, acc_old = m[:, c0:], acc[:, c0:]
            m_new = jnp.maximum(m_old, m_blk)
            acc_new = jnp.exp2(m_old - m_new) * acc_old + _dot(vt, jnp.exp2((s - m_new).astype(BF16)))
            if c0:
                m_new = jnp.concatenate([m[:, :c0], m_new], axis=1)
                acc_new = jnp.concatenate([acc[:, :c0], acc_new], axis=1)
            m, acc = m_new, acc_new
        if k0 + kl == (qi + 1) * tq:
            o = (acc[:MLA_V] * (1.0 / acc[MLA_V:MLA_V + 1])).T
            gmul = gmul_ref[qi * tq:(qi + 1) * tq, hd * MLA_V:(hd + 1) * MLA_V].astype(F32)
            y_ref[qi * tq:(qi + 1) * tq, hd * MLA_V:(hd + 1) * MLA_V] = (o * gmul).astype(BF16)


def _mla_attn(q, k, vt, gmul, batch, seq):
    n = q.shape[0]
    hps = ATT_HEADS_PER_STEP
    return pl.pallas_call(
        _mla_attn_kernel,
        grid=(batch, MLA_HEADS // hps),
        in_specs=[pl.BlockSpec((seq, hps * MLA_HEAD_PAD), lambda b, h: (b, h)),
                  pl.BlockSpec((seq, hps * MLA_HEAD_PAD), lambda b, h: (b, h)),
                  pl.BlockSpec((hps * MLA_V, seq), lambda b, h: (h, b)),
                  pl.BlockSpec((seq, hps * MLA_V), lambda b, h: (b, h))],
        out_specs=pl.BlockSpec((seq, hps * MLA_V), lambda b, h: (b, h)),
        out_shape=jax.ShapeDtypeStruct((n, MLA_V_W), BF16),
        compiler_params=_params("parallel", "parallel"),
        name="mla_attn",
    )(q, k, vt, gmul)


def _final_proj_kernel(y_ref, w_ref, h_ref, g_ref, o_ref):
    o_ref[...] = _rms(h_ref[...] + _dot(y_ref[...], w_ref[...])) * g_ref[...]


def _final_proj(y, w_out, h, final_gain):
    n, width = y.shape
    rows = 2 * ROW_TILE
    row_block = lambda w: pl.BlockSpec((rows, w), lambda i: (i, 0))
    return pl.pallas_call(
        _final_proj_kernel,
        grid=(n // rows,),
        in_specs=[row_block(width), _resident((width, D_MODEL)), row_block(D_MODEL), _resident((1, D_MODEL))],
        out_specs=row_block(D_MODEL),
        out_shape=jax.ShapeDtypeStruct((n, D_MODEL), F32),
        compiler_params=_params("parallel"),
        name="final_proj",
    )(y, w_out.astype(BF16), h, final_gain.reshape(1, D_MODEL))


def kernel(x, positions, l0_norm, l0_ret_w_in, l0_ret_gn, l0_ret_w_out, l1_norm, l1_mla_w_in, l1_mla_q_norm, l1_mla_w_q_b, l1_mla_kv_norm, l1_mla_w_kv_b, l1_mla_w_out, l2_norm, l2_ret_w_in, l2_ret_gn, l2_ret_w_out, l3_norm, l3_mla_w_in, l3_mla_q_norm, l3_mla_w_q_b, l3_mla_kv_norm, l3_mla_w_kv_b, l3_mla_w_out, final_norm):
    batch, seq, d = x.shape
    assert d == D_MODEL and seq % ATT_Q_TILE == 0 and ATT_Q_TILE % ATT_K_TILE == 0
    assert seq % RET_ROWS == 0 and RET_ROWS % RET_BLOCK == 0 and RET_HEADS > 1 and (batch * seq) % ROW_TILE == 0
    h = x.reshape(batch * seq, d)
    cos_r, sin_r, cos_m, sin_m = _rope_tables(positions)

    def retention(h, prev, norm, w_in, gn):
        return _ret_layer(h, prev, norm, w_in, gn, cos_r, sin_r, batch, seq)

    def latent_attention(h, prev, norm, w_in, q_norm, w_q_b, kv_norm, w_kv_b):
        h, q, k, vt, gmul = _mla_inproj(h, prev, norm, w_in, q_norm, w_q_b, kv_norm, w_kv_b, cos_m, sin_m)
        return h, _mla_attn(q, k, vt, gmul, batch, seq)

    h, y = retention(h, None, l0_norm, l0_ret_w_in, l0_ret_gn)
    h, y = latent_attention(h, (y, l0_ret_w_out), l1_norm, l1_mla_w_in, l1_mla_q_norm, l1_mla_w_q_b,
                            l1_mla_kv_norm, l1_mla_w_kv_b)
    h, y = retention(h, (y, l1_mla_w_out), l2_norm, l2_ret_w_in, l2_ret_gn)
    h, y = latent_attention(h, (y, l2_ret_w_out), l3_norm, l3_mla_w_in, l3_mla_q_norm, l3_mla_w_q_b,
                            l3_mla_kv_norm, l3_mla_w_kv_b)
    h = _final_proj(y, l3_mla_w_out, h, final_norm)
    return h.reshape(batch, seq, d)
```
